```python
import jax
import jax.numpy as jnp
from jax import lax
import numpy as np

D_MODEL = 2048
BATCH = 4
SEQ = 2048
DEPTH = 4
DEC_BATCH = 32
DEC_SEQ = 4
PAST_LEN = 16384
PAGE_SIZE = 128

N_MIXERS = 3
N_HEADS = 16
N_KV_HEADS = 4
HEAD_DIM = 128
GROUP = N_HEADS // N_KV_HEADS
Q_DIM = N_HEADS * HEAD_DIM
KV_DIM = N_KV_HEADS * HEAD_DIM
QKV_DIM = Q_DIM + 2 * KV_DIM
ROPE_THETA = 500000.0
ROPE_DIM = HEAD_DIM // 4
IDX_HEADS = 16
IDX_DIM = 64
IDX_ROPE_DIM = IDX_DIM // 4
IDX_PROJ = IDX_HEADS * IDX_DIM + IDX_DIM + IDX_HEADS
DSA_TOPK_MAX = 256
DSA_Q_BLOCK = 128
MOBA_BLOCK = 256
MOBA_TOPK = 3
MOBA_Q_CHUNK = 16
WINDOW = 128
D_FF = -(-(8 * D_MODEL) // (3 * 256)) * 256
N_LAYERS_A = (DEPTH + N_MIXERS - 1) // N_MIXERS
N_LAYERS_C = DEPTH // N_MIXERS
RMS_EPS = 1e-6
NEG_INF = -1e30
ATTN_SCALE = HEAD_DIM ** -0.5

kernel_name = 'hybrid_dsa_moba_swa_decoder_step'


def rms_norm(x, g):
    xf = x.astype(jnp.float32)
    y = xf * lax.rsqrt(jnp.mean(xf * xf, axis=-1, keepdims=True) + RMS_EPS)
    return (y * g.astype(jnp.float32)).astype(x.dtype)


def layer_norm(x, g, b):
    xf = x.astype(jnp.float32)
    xc = xf - jnp.mean(xf, axis=-1, keepdims=True)
    var = jnp.mean(xc * xc, axis=-1, keepdims=True)
    y = xc * lax.rsqrt(var + RMS_EPS) * g.astype(jnp.float32) + b.astype(jnp.float32)
    return y.astype(x.dtype)


def rope_partial(x, pos, rot_dim):
    half = rot_dim // 2
    inv_freq = ROPE_THETA ** (-jnp.arange(half, dtype=jnp.float32) / half)
    ang = pos.astype(jnp.float32)[:, None] * inv_freq[None, :]
    cos = jnp.cos(ang)[:, None, :]
    sin = jnp.sin(ang)[:, None, :]
    xf = x.astype(jnp.float32)
    x1 = xf[..., :half]
    x2 = xf[..., half:rot_dim]
    out = jnp.concatenate([x1 * cos - x2 * sin, x2 * cos + x1 * sin, xf[..., rot_dim:]], axis=-1)
    return out.astype(x.dtype)


def project_qkv(h, w, q_g, k_g, pos):
    B, T, _ = h.shape
    qkv = jnp.einsum('btd,de->bte', h, w)
    q = qkv[..., :Q_DIM].reshape(B, T, N_HEADS, HEAD_DIM)
    k = qkv[..., Q_DIM:Q_DIM + KV_DIM].reshape(B, T, N_KV_HEADS, HEAD_DIM)
    v = qkv[..., Q_DIM + KV_DIM:].reshape(B, T, N_KV_HEADS, HEAD_DIM)
    q = rope_partial(rms_norm(q, q_g), pos, ROPE_DIM)
    k = rope_partial(rms_norm(k, k_g), pos, ROPE_DIM)
    return q, k, v


def indexer_project(h, w, ln_g, ln_b, pos):
    B, T, _ = h.shape
    proj = jnp.einsum('btd,de->bte', h, w)
    qi = proj[..., :IDX_HEADS * IDX_DIM].reshape(B, T, IDX_HEADS, IDX_DIM)
    ki = proj[..., IDX_HEADS * IDX_DIM:IDX_HEADS * IDX_DIM + IDX_DIM]
    wi = proj[..., IDX_HEADS * IDX_DIM + IDX_DIM:] * (IDX_HEADS ** -0.5 * IDX_DIM ** -0.5)
    qi = rope_partial(qi, pos, IDX_ROPE_DIM)
    ki = rope_partial(layer_norm(ki, ln_g, ln_b)[:, :, None, :], pos, IDX_ROPE_DIM)[:, :, 0, :]
    return qi, ki, wi


def index_scores(qi, ki, wi):
    dots = jnp.einsum('bqhd,bld->bqhl', qi.astype(jnp.float32), ki.astype(jnp.float32))
    return jnp.einsum('bqhl,bqh->bql', jax.nn.relu(dots), wi.astype(jnp.float32))


def gathered_attend(q, kg, vg, valid):
    B, Q = q.shape[0], q.shape[1]
    qg = q.reshape(B, Q, N_KV_HEADS, GROUP, HEAD_DIM).astype(jnp.float32)
    s = jnp.einsum('bqgnd,bqkgd->bqgnk', qg, kg.astype(jnp.float32)) * ATTN_SCALE
    s = jnp.where(valid[:, :, None, None, :], s, NEG_INF)
    p = jax.nn.softmax(s, axis=-1)
    o = jnp.einsum('bqgnk,bqkgd->bqgnd', p, vg.astype(jnp.float32))
    return o.reshape(B, Q, N_HEADS, HEAD_DIM).astype(q.dtype)


def dsa_prompt(q, k, v, qi, ki, wi):
    B, T = q.shape[0], q.shape[1]
    topk = min(DSA_TOPK_MAX, T // 4)
    n_blk = T // DSA_Q_BLOCK
    pos_k = jnp.arange(T, dtype=jnp.int32)
    take_rows = jax.vmap(lambda rows, idx: rows[idx])

    def block(i):
        s0 = i * DSA_Q_BLOCK
        qb = lax.dynamic_slice_in_dim(q, s0, DSA_Q_BLOCK, axis=1)
        qib = lax.dynamic_slice_in_dim(qi, s0, DSA_Q_BLOCK, axis=1)
        wib = lax.dynamic_slice_in_dim(wi, s0, DSA_Q_BLOCK, axis=1)
        pos_q = s0 + jnp.arange(DSA_Q_BLOCK, dtype=jnp.int32)
        causal = pos_k[None, None, :] <= pos_q[None, :, None]
        score = jnp.where(causal, index_scores(qib, ki, wib), NEG_INF)
        _, sel = lax.top_k(score, topk)
        valid = sel <= pos_q[None, :, None]
        return gathered_attend(qb, take_rows(k, sel), take_rows(v, sel), valid)

    out = lax.map(block, jnp.arange(n_blk, dtype=jnp.int32))
    return out.transpose(1, 0, 2, 3, 4).reshape(q.shape)


def dsa_sample(q, k_new, v_new, qi, ki_new, wi, cache_k, cache_v, cache_ik, page_table):
    DB, S = q.shape[0], q.shape[1]
    n_pages = page_table.shape[1]
    L = PAST_LEN + S
    topk = min(DSA_TOPK_MAX, L // 4)
    ki_past = cache_ik[page_table].reshape(DB, n_pages * PAGE_SIZE, IDX_DIM)
    ki_all = jnp.concatenate([ki_past.astype(ki_new.dtype), ki_new], axis=1)
    pos_q = PAST_LEN + jnp.arange(S, dtype=jnp.int32)
    pos_k = jnp.arange(L, dtype=jnp.int32)
    causal = pos_k[None, None, :] <= pos_q[None, :, None]
    score = jnp.where(causal, index_scores(qi, ki_all, wi), NEG_INF)
    _, sel = lax.top_k(score, topk)
    valid = sel <= pos_q[None, :, None]
    from_past = (sel < PAST_LEN)[..., None, None]
    page_idx = jnp.minimum(sel // PAGE_SIZE, n_pages - 1)
    phys = page_table[jnp.arange(DB, dtype=jnp.int32)[:, None, None], page_idx]
    off = sel % PAGE_SIZE
    new_idx = jnp.clip(sel - PAST_LEN, 0, S - 1)
    take_rows = jax.vmap(lambda rows, idx: rows[idx])
    kg = jnp.where(from_past, cache_k[phys, off].astype(k_new.dtype), take_rows(k_new, new_idx))
    vg = jnp.where(from_past, cache_v[phys, off].astype(v_new.dtype), take_rows(v_new, new_idx))
    return gathered_attend(q, kg, vg, valid)


def moba_attend_seq(q, pos_q, k_all, v_all):
    Tq = q.shape[0]
    n_blk = k_all.shape[0] // MOBA_BLOCK
    kb = k_all.reshape(n_blk, MOBA_BLOCK, N_KV_HEADS, HEAD_DIM).transpose(0, 2, 1, 3)
    vb = v_all.reshape(n_blk, MOBA_BLOCK, N_KV_HEADS, HEAD_DIM).transpose(0, 2, 1, 3)
    k_mean = jnp.mean(kb.astype(jnp.float32), axis=2)
    qg = q.reshape(Tq, N_KV_HEADS, GROUP, HEAD_DIM).astype(jnp.float32)
    gate = jnp.einsum('tgnd,bgd->tgnb', qg, k_mean)
    own = pos_q // MOBA_BLOCK
    fully_past = jnp.arange(n_blk, dtype=jnp.int32)[None, :] < own[:, None]
    gate = jnp.where(fully_past[:, None, None, :], gate, NEG_INF)
    n_sel = min(MOBA_TOPK, n_blk)
    _, sel = lax.top_k(gate, n_sel)
    sel_ok = sel < own[:, None, None, None]
    own_b = jnp.broadcast_to(own[:, None, None, None], (Tq, N_KV_HEADS, GROUP, 1))
    blocks = jnp.concatenate([sel, own_b], axis=-1)
    ok = jnp.concatenate([sel_ok, jnp.ones(own_b.shape, dtype=bool)], axis=-1)
    g_idx = jnp.arange(N_KV_HEADS, dtype=jnp.int32)[None, :, None, None]
    kg = kb[blocks, g_idx].astype(jnp.float32)
    vg = vb[blocks, g_idx].astype(jnp.float32)
    key_pos = blocks[..., None] * MOBA_BLOCK + jnp.arange(MOBA_BLOCK, dtype=jnp.int32)
    mask = ok[..., None] & (key_pos <= pos_q[:, None, None, None, None])
    s = jnp.einsum('tgnd,tgnmkd->tgnmk', qg, kg) * ATTN_SCALE
    s = jnp.where(mask, s, NEG_INF).reshape(Tq, N_KV_HEADS, GROUP, -1)
    p = jax.nn.softmax(s, axis=-1)
    o = jnp.einsum('tgnj,tgnjd->tgnd', p, vg.reshape(Tq, N_KV_HEADS, GROUP, -1, HEAD_DIM))
    return o.reshape(Tq, N_HEADS, HEAD_DIM).astype(q.dtype)


def moba_prompt(q, k, v):
    B, T = q.shape[0], q.shape[1]
    l_pad = -(-T // MOBA_BLOCK) * MOBA_BLOCK
    pad = ((0, 0), (0, l_pad - T), (0, 0), (0, 0))
    kp = jnp.pad(k, pad)
    vp = jnp.pad(v, pad)
    n_chunk = T // MOBA_Q_CHUNK

    def step(idx):
        b = idx // n_chunk
        s0 = (idx % n_chunk) * MOBA_Q_CHUNK
        qc = lax.dynamic_slice(q, (b, s0, 0, 0), (1, MOBA_Q_CHUNK, N_HEADS, HEAD_DIM))[0]
        pos_q = s0 + jnp.arange(MOBA_Q_CHUNK, dtype=jnp.int32)
        return moba_attend_seq(qc, pos_q, kp[b], vp[b])

    out = lax.map(step, jnp.arange(B * n_chunk, dtype=jnp.int32))
    return out.reshape(q.shape)


def moba_sample(q, k_new, v_new, cache_k, cache_v, page_table):
    S = q.shape[1]
    L = PAST_LEN + S
    l_pad = -(-L // MOBA_BLOCK) * MOBA_BLOCK
    pos_q = PAST_LEN + jnp.arange(S, dtype=jnp.int32)

    def one(args):
        qb, kn, vn, pt = args
        past_len = pt.shape[0] * PAGE_SIZE
        zpad = jnp.zeros((l_pad - L, N_KV_HEADS, HEAD_DIM), kn.dtype)
        k_all = jnp.concatenate([cache_k[pt].reshape(past_len, N_KV_HEADS, HEAD_DIM).astype(kn.dtype), kn, zpad], axis=0)
        v_all = jnp.concatenate([cache_v[pt].reshape(past_len, N_KV_HEADS, HEAD_DIM).astype(vn.dtype), vn, zpad], axis=0)
        return moba_attend_seq(qb, pos_q, k_all, v_all)

    return lax.map(one, (q, k_new, v_new, page_table))


def sink_softmax(s, sink):
    m = jnp.maximum(jnp.max(s, axis=-1, keepdims=True), sink)
    e = jnp.exp(s - m)
    return e / (jnp.sum(e, axis=-1, keepdims=True) + jnp.exp(sink - m))


def swa_prompt(q, k, v, sinks):
    B, T = q.shape[0], q.shape[1]
    nb = T // WINDOW
    qb = q.reshape(B, nb, WINDOW, N_KV_HEADS, GROUP, HEAD_DIM).astype(jnp.float32)
    kb = k.reshape(B, nb, WINDOW, N_KV_HEADS, HEAD_DIM).astype(jnp.float32)
    vb = v.reshape(B, nb, WINDOW, N_KV_HEADS, HEAD_DIM).astype(jnp.float32)
    kband = jnp.concatenate([jnp.roll(kb, 1, axis=1), kb], axis=2)
    vband = jnp.concatenate([jnp.roll(vb, 1, axis=1), vb], axis=2)
    q_rel = jnp.arange(WINDOW, dtype=jnp.int32)[:, None]
    k_rel = jnp.arange(2 * WINDOW, dtype=jnp.int32)[None, :] - WINDOW
    diff = q_rel - k_rel
    band = (diff >= 0) & (diff < WINDOW)
    blk_ok = (jnp.arange(nb, dtype=jnp.int32)[:, None, None] > 0) | (k_rel >= 0)[None]
    mask = band[None] & blk_ok
    s = jnp.einsum('bnqgmd,bnkgd->bngmqk', qb, kband) * ATTN_SCALE
    s = jnp.where(mask[None, :, None, None, :, :], s, NEG_INF)
    sink = sinks.astype(jnp.float32).reshape(N_KV_HEADS, GROUP)[None, None, :, :, None, None]
    p = sink_softmax(s, sink)
    o = jnp.einsum('bngmqk,bnkgd->bnqgmd', p, vband)
    return o.reshape(B, T, N_HEADS, HEAD_DIM).astype(q.dtype)


def swa_sample(q, k_new, v_new, buf_k, buf_v, sinks):
    DB, S = q.shape[0], q.shape[1]
    k_all = jnp.concatenate([buf_k.astype(k_new.dtype), k_new], axis=1).astype(jnp.float32)
    v_all = jnp.concatenate([buf_v.astype(v_new.dtype), v_new], axis=1).astype(jnp.float32)
    pos_q = PAST_LEN + jnp.arange(S, dtype=jnp.int32)
    pos_k = PAST_LEN - WINDOW + jnp.arange(WINDOW + S, dtype=jnp.int32)
    diff = pos_q[:, None] - pos_k[None, :]
    mask = (diff >= 0) & (diff < WINDOW)
    qg = q.reshape(DB, S, N_KV_HEADS, GROUP, HEAD_DIM).astype(jnp.float32)
    s = jnp.einsum('bqgmd,bkgd->bgmqk', qg, k_all) * ATTN_SCALE
    s = jnp.where(mask[None, None, None], s, NEG_INF)
    sink = sinks.astype(jnp.float32).reshape(N_KV_HEADS, GROUP)[None, :, :, None, None]
    p = sink_softmax(s, sink)
    o = jnp.einsum('bgmqk,bkgd->bqgmd', p, v_all)
    return o.reshape(DB, S, N_HEADS, HEAD_DIM).astype(q.dtype)


def out_proj(o, w):
    return jnp.einsum('bthd,hde->bte', o, w.reshape(N_HEADS, HEAD_DIM, D_MODEL))


def swiglu_ffn(h, w_gu, w_d):
    gu = jnp.einsum('btd,df->btf', h, w_gu)
    return jnp.einsum('btf,fd->btd', jax.nn.silu(gu[..., :D_FF]) * gu[..., D_FF:], w_d)


def setup_inputs(seed: int = 0) -> dict:
    key = jax.random.key(seed)
    ks = jax.random.split(key, 32)
    f32 = jnp.float32
    n_pages = PAST_LEN // PAGE_SIZE
    pages_used = DEC_BATCH * n_pages
    pool_pages = pages_used + pages_used // 4

    def nrm(k, shape, scale=1.0):
        return jax.random.normal(k, shape, f32) * scale

    kv_shape = (pool_pages, PAGE_SIZE, N_KV_HEADS, HEAD_DIM)
    ik_shape = (pool_pages, PAGE_SIZE, IDX_DIM)
    win_shape = (DEC_BATCH, WINDOW, N_KV_HEADS, HEAD_DIM)
    page_table = jax.random.permutation(ks[12], pool_pages)[:pages_used].reshape(DEC_BATCH, n_pages).astype(jnp.int32)
    return {
        'x_prompt': nrm(ks[0], (BATCH, SEQ, D_MODEL)),
        'x_sample': nrm(ks[1], (DEC_BATCH, DEC_SEQ, D_MODEL)),
        'cache_k_0': nrm(ks[2], kv_shape),
        'cache_v_0': nrm(ks[3], kv_shape),
        'cache_ik_0': nrm(ks[4], ik_shape),
        'cache_k_1': nrm(ks[5], kv_shape),
        'cache_v_1': nrm(ks[6], kv_shape),
        'state_wk_2': nrm(ks[7], win_shape),
        'state_wv_2': nrm(ks[8], win_shape),
        'cache_k_3': nrm(ks[9], kv_shape),
        'cache_v_3': nrm(ks[10], kv_shape),
        'cache_ik_3': nrm(ks[11], ik_shape),
        'page_table': page_table,
        'ln_attn': 1.0 + nrm(ks[13], (DEPTH, D_MODEL), 0.01),
        'w_qkv': nrm(ks[14], (DEPTH, D_MODEL, QKV_DIM), D_MODEL ** -0.5),
        'q_norm': 1.0 + nrm(ks[15], (DEPTH, HEAD_DIM), 0.01),
        'k_norm': 1.0 + nrm(ks[16], (DEPTH, HEAD_DIM), 0.01),
        'w_o': nrm(ks[17], (DEPTH, Q_DIM, D_MODEL), Q_DIM ** -0.5),
        'w_idx': nrm(ks[18], (N_LAYERS_A, D_MODEL, IDX_PROJ), D_MODEL ** -0.5),
        'idx_ln_g': 1.0 + nrm(ks[19], (N_LAYERS_A, IDX_DIM), 0.01),
        'idx_ln_b': nrm(ks[20], (N_LAYERS_A, IDX_DIM), 0.01),
        'sinks': nrm(ks[21], (N_LAYERS_C, N_HEADS), 0.5),
        'ln_ffn': 1.0 + nrm(ks[22], (DEPTH, D_MODEL), 0.01),
        'w_gate_up': nrm(ks[23], (DEPTH, D_MODEL, 2 * D_FF), D_MODEL ** -0.5),
        'w_down': nrm(ks[24], (DEPTH, D_FF, D_MODEL), D_FF ** -0.5),
    }


def reference(x_prompt, x_sample, cache_k_0, cache_v_0, cache_ik_0, cache_k_1, cache_v_1,
              state_wk_2, state_wv_2, cache_k_3, cache_v_3, cache_ik_3, page_table,
              ln_attn, w_qkv, q_norm, k_norm, w_o, w_idx, idx_ln_g, idx_ln_b, sinks,
              ln_ffn, w_gate_up, w_down):
    layer_caches = [(cache_k_0, cache_v_0, cache_ik_0), (cache_k_1, cache_v_1),
                    (state_wk_2, state_wv_2), (cache_k_3, cache_v_3, cache_ik_3)]
    pos_p = jnp.arange(x_prompt.shape[1], dtype=jnp.int32)
    pos_s = PAST_LEN + jnp.arange(x_sample.shape[1], dtype=jnp.int32)
    xp, xs = x_prompt, x_sample
    new_states = []
    for i in range(DEPTH):
        kind = i % N_MIXERS
        j = i // N_MIXERS
        hp = rms_norm(xp, ln_attn[i])
        hs = rms_norm(xs, ln_attn[i])
        qp, kp, vp = project_qkv(hp, w_qkv[i], q_norm[i], k_norm[i], pos_p)
        qs, ks_, vs = project_qkv(hs, w_qkv[i], q_norm[i], k_norm[i], pos_s)
        if kind == 0:
            ck, cv, cik = layer_caches[i]
            qip, kip, wip = indexer_project(hp, w_idx[j], idx_ln_g[j], idx_ln_b[j], pos_p)
            qis, kis, wis = indexer_project(hs, w_idx[j], idx_ln_g[j], idx_ln_b[j], pos_s)
            op = dsa_prompt(qp, kp, vp, qip, kip, wip)
            os_ = dsa_sample(qs, ks_, vs, qis, kis, wis, ck, cv, cik, page_table)
            new_states.append((kp, vp, kip, ks_, vs, kis))
        elif kind == 1:
            ck, cv = layer_caches[i]
            op = moba_prompt(qp, kp, vp)
            os_ = moba_sample(qs, ks_, vs, ck, cv, page_table)
            new_states.append((kp, vp, ks_, vs))
        else:
            bk, bv = layer_caches[i]
            op = swa_prompt(qp, kp, vp, sinks[j])
            os_ = swa_sample(qs, ks_, vs, bk, bv, sinks[j])
            new_states.append((kp[:, -WINDOW:], vp[:, -WINDOW:],
                               jnp.concatenate([bk.astype(ks_.dtype), ks_], axis=1)[:, -WINDOW:],
                               jnp.concatenate([bv.astype(vs.dtype), vs], axis=1)[:, -WINDOW:]))
        xp = xp + out_proj(op, w_o[i])
        xs = xs + out_proj(os_, w_o[i])
        xp = xp + swiglu_ffn(rms_norm(xp, ln_ffn[i]), w_gate_up[i], w_down[i])
        xs = xs + swiglu_ffn(rms_norm(xs, ln_ffn[i]), w_gate_up[i], w_down[i])
    k_p0, v_p0, ik_p0, k_s0, v_s0, ik_s0 = new_states[0]
    k_p1, v_p1, k_s1, v_s1 = new_states[1]
    wk_p2, wv_p2, wk_s2, wv_s2 = new_states[2]
    k_p3, v_p3, ik_p3, k_s3, v_s3, ik_s3 = new_states[3]
    return (xp, xs, k_p0, v_p0, ik_p0, k_s0, v_s0, ik_s0, k_p1, v_p1, k_s1, v_s1,
            wk_p2, wv_p2, wk_s2, wv_s2, k_p3, v_p3, ik_p3, k_s3, v_s3, ik_s3)
```

```python
import functools

import jax
import jax.numpy as jnp
from jax import lax
from jax.experimental import pallas as pl
from jax.experimental.pallas import tpu as pltpu

F32 = jnp.float32
BF16 = jnp.bfloat16

N_HEADS = 16
N_KV_HEADS = 4
HEAD_DIM = 128
GROUP = N_HEADS // N_KV_HEADS
Q_DIM = N_HEADS * HEAD_DIM
KV_DIM = N_KV_HEADS * HEAD_DIM
QKV_DIM = Q_DIM + 2 * KV_DIM
ROPE_THETA = 500000.0
ROPE_DIM = HEAD_DIM // 4
IDX_HEADS = 16
IDX_DIM = 64
IDX_ROPE_DIM = IDX_DIM // 4
IDX_Q = IDX_HEADS * IDX_DIM
IDX_PROJ = IDX_Q + IDX_DIM + IDX_HEADS
IDX_PROJ_PAD = IDX_Q + 128
DSA_TOPK_MAX = 256
MOBA_BLOCK = 256
MOBA_TOPK = 3
WINDOW = 128
PAGE_SIZE = 128
RMS_EPS = 1e-6
NEG_INF = -1e30
ATTN_SCALE = HEAD_DIM ** -0.5
INT32_MIN = -(2 ** 31)

LANES = 128
VMEM_LIMIT = 56 * 1024 * 1024
ROW_TILE_MAX = 640
QKV_COL_TILE = 512
FFN_COL_TILE = 512
ATTN_Q_TILE = 128
DSA_KEY_CHUNK = 512
PAGES_PER_STEP = 8
MOBA_SUM_PAGES = 16

_NT = (((1,), (1,)), ((), ()))


def _params(*sem):
    return pltpu.CompilerParams(dimension_semantics=sem, vmem_limit_bytes=VMEM_LIMIT)


def _log2(v):
    assert v > 0 and v & (v - 1) == 0, v
    return v.bit_length() - 1


def _div_pow2(x, d):
    return x >> _log2(d)


def _mod_pow2(x, d):
    _log2(d)
    return x & (d - 1)


def _per_head_rows(sink_ref, g, row_head):
    out = jnp.full(row_head.shape, sink_ref[g * GROUP + GROUP - 1], F32)
    for n in range(GROUP - 1):
        out = jnp.where(row_head == n, sink_ref[g * GROUP + n], out)
    return out


def _row_tile(n):
    best = None
    for t in range(LANES, ROW_TILE_MAX + 1, LANES):
        if n % t == 0:
            best = t
    assert best is not None, n
    return best


def _rope_tables(pos, half, period):
    inv_freq = ROPE_THETA ** (-jnp.arange(half, dtype=F32) / half)
    ang = pos.astype(F32)[:, None] * inv_freq[None, :]
    cos = jnp.cos(ang)
    sin = jnp.sin(ang)
    r = jnp.arange(LANES) % period
    lo = r < half
    hi = (r >= half) & (r < 2 * half)
    idx = jnp.clip(jnp.where(lo, r, r - half), 0, half - 1)
    cg = cos[:, idx]
    sg = sin[:, idx]
    return (jnp.where(lo | hi, cg, 1.0).astype(F32),
            jnp.where(lo, -sg, 0.0).astype(F32),
            jnp.where(hi, sg, 0.0).astype(F32))


def _rope(y, cos, sin_a, sin_b, half):
    return y * cos + pltpu.roll(y, LANES - half, 1) * sin_a + pltpu.roll(y, half, 1) * sin_b


def _rms_rows_bf16(x_ref, ln_ref):
    xf = x_ref[...]
    ms = jnp.mean(xf * xf, axis=-1, keepdims=True)
    return (xf * lax.rsqrt(ms + RMS_EPS) * ln_ref[...]).astype(BF16)


def _qkv_kernel(x_ref, ln_ref, w_ref, g_ref, cos_ref, sa_ref, sb_ref, o_ref, h_ref, *, n_norm_tiles):
    j = pl.program_id(1)

    @pl.when(j == 0)
    def _():
        h_ref[...] = _rms_rows_bf16(x_ref, ln_ref)

    acc = jnp.dot(h_ref[...], w_ref[...], preferred_element_type=F32)

    @pl.when(j < n_norm_tiles)
    def _():
        cos = cos_ref[...]
        sa = sa_ref[...]
        sb = sb_ref[...]
        for hh in range(acc.shape[1] // HEAD_DIM):
            sl = slice(hh * HEAD_DIM, (hh + 1) * HEAD_DIM)
            a = acc[:, sl]
            ms = jnp.mean(a * a, axis=-1, keepdims=True)
            y = a * lax.rsqrt(ms + RMS_EPS) * g_ref[:, sl]
            o_ref[:, sl] = _rope(y, cos, sa, sb, ROPE_DIM // 2)

    @pl.when(j >= n_norm_tiles)
    def _():
        o_ref[...] = acc


def _qkv_proj(x, ln, w, gain, tabs, tm):
    n, d = x.shape
    tn = QKV_COL_TILE
    tab_spec = pl.BlockSpec((tm, LANES), lambda i, j: (i, 0))
    return pl.pallas_call(
        functools.partial(_qkv_kernel, n_norm_tiles=(Q_DIM + KV_DIM) // tn),
        out_shape=jax.ShapeDtypeStruct((n, QKV_DIM), F32),
        grid=(n // tm, QKV_DIM // tn),
        in_specs=[pl.BlockSpec((tm, d), lambda i, j: (i, 0)),
                  pl.BlockSpec((1, d), lambda i, j: (0, 0)),
                  pl.BlockSpec((d, tn), lambda i, j: (0, j)),
                  pl.BlockSpec((1, tn), lambda i, j: (0, j)),
                  tab_spec, tab_spec, tab_spec],
        out_specs=pl.BlockSpec((tm, tn), lambda i, j: (i, j)),
        scratch_shapes=[pltpu.VMEM((tm, d), BF16)],
        compiler_params=_params("parallel", "arbitrary"),
    )(x, ln, w, gain, *tabs)


def _idx_kernel(x_ref, ln_ref, w_ref, lg_ref, lb_ref, cq_ref, saq_ref, sbq_ref,
                ck_ref, sak_ref, sbk_ref, qi_ref, kw_ref):
    h = _rms_rows_bf16(x_ref, ln_ref)
    acc = jnp.dot(h, w_ref[...], preferred_element_type=F32)
    half = IDX_ROPE_DIM // 2
    cq = cq_ref[...]
    saq = saq_ref[...]
    sbq = sbq_ref[...]
    for c in range(IDX_Q // LANES):
        sl = slice(c * LANES, (c + 1) * LANES)
        qi_ref[:, sl] = _rope(acc[:, sl], cq, saq, sbq, half)
    a = acc[:, IDX_Q:IDX_Q + LANES]
    lane = lax.broadcasted_iota(jnp.int32, a.shape, 1)
    is_key = lane < IDX_DIM
    mu = jnp.sum(jnp.where(is_key, a, 0.0), axis=-1, keepdims=True) * (1.0 / IDX_DIM)
    xc = a - mu
    var = jnp.sum(jnp.where(is_key, xc * xc, 0.0), axis=-1, keepdims=True) * (1.0 / IDX_DIM)
    y = xc * lax.rsqrt(var + RMS_EPS) * lg_ref[...] + lb_ref[...]
    kn = jnp.where(is_key, y, a * (IDX_HEADS ** -0.5 * IDX_DIM ** -0.5))
    kw_ref[...] = _rope(kn, ck_ref[...], sak_ref[...], sbk_ref[...], half)


def _idx_proj(x, ln, w, lg, lb, tabs_q, tabs_k, tm):
    n, d = x.shape
    row = lambda i: (i, 0)
    fix = lambda i: (0, 0)
    tab_spec = pl.BlockSpec((tm, LANES), row)
    return pl.pallas_call(
        _idx_kernel,
        out_shape=(jax.ShapeDtypeStruct((n, IDX_Q), F32), jax.ShapeDtypeStruct((n, LANES), F32)),
        grid=(n // tm,),
        in_specs=[pl.BlockSpec((tm, d), row), pl.BlockSpec((1, d), fix),
                  pl.BlockSpec((d, IDX_PROJ_PAD), fix),
                  pl.BlockSpec((1, LANES), fix), pl.BlockSpec((1, LANES), fix)] + [tab_spec] * 6,
        out_specs=(pl.BlockSpec((tm, IDX_Q), row), pl.BlockSpec((tm, LANES), row)),
        compiler_params=_params("parallel"),
    )(x, ln, w, lg, lb, *tabs_q, *tabs_k)


def _oproj_kernel(o_ref, w_ref, x_ref, out_ref, ob_ref):
    @pl.when(pl.program_id(1) == 0)
    def _():
        ob_ref[...] = o_ref[...].astype(BF16)

    out_ref[...] = x_ref[...] + jnp.dot(ob_ref[...], w_ref[...], preferred_element_type=F32)


def _out_proj(o, w, x, tm):
    n, d = x.shape
    tn = min(QKV_COL_TILE, d)
    return pl.pallas_call(
        _oproj_kernel,
        out_shape=jax.ShapeDtypeStruct((n, d), F32),
        grid=(n // tm, d // tn),
        in_specs=[pl.BlockSpec((tm, Q_DIM), lambda i, j: (i, 0)),
                  pl.BlockSpec((Q_DIM, tn), lambda i, j: (0, j)),
                  pl.BlockSpec((tm, tn), lambda i, j: (i, j))],
        out_specs=pl.BlockSpec((tm, tn), lambda i, j: (i, j)),
        scratch_shapes=[pltpu.VMEM((tm, Q_DIM), BF16)],
        compiler_params=_params("parallel", "arbitrary"),
    )(o, w, x)


def _ffn_kernel(x_ref, ln_ref, wg_ref, wu_ref, wd_ref, o_ref, h_ref, acc_ref):
    f = pl.program_id(1)

    @pl.when(f == 0)
    def _():
        h_ref[...] = _rms_rows_bf16(x_ref, ln_ref)
        acc_ref[...] = jnp.zeros_like(acc_ref)

    h = h_ref[...]
    g = jnp.dot(h, wg_ref[...], preferred_element_type=F32)
    u = jnp.dot(h, wu_ref[...], preferred_element_type=F32)
    a = (g * jax.nn.sigmoid(g)) * u
    acc_ref[...] += jnp.dot(a.astype(BF16), wd_ref[...], preferred_element_type=F32)

    @pl.when(f == pl.num_programs(1) - 1)
    def _():
        o_ref[...] = x_ref[...] + acc_ref[...]


def _ffn(x, ln, w_gu, w_d, tm):
    n, d = x.shape
    d_ff = w_d.shape[0]
    tf = min(FFN_COL_TILE, d_ff)
    nf = d_ff // tf
    return pl.pallas_call(
        _ffn_kernel,
        out_shape=jax.ShapeDtypeStruct((n, d), F32),
        grid=(n // tm, nf),
        in_specs=[pl.BlockSpec((tm, d), lambda i, f: (i, 0)),
                  pl.BlockSpec((1, d), lambda i, f: (0, 0)),
                  pl.BlockSpec((d, tf), lambda i, f: (0, f)),
                  pl.BlockSpec((d, tf), lambda i, f: (0, f + nf)),
                  pl.BlockSpec((tf, d), lambda i, f: (f, 0))],
        out_specs=pl.BlockSpec((tm, d), lambda i, f: (i, 0)),
        scratch_shapes=[pltpu.VMEM((tm, d), BF16), pltpu.VMEM((tm, d), F32)],
        compiler_params=_params("parallel", "arbitrary"),
    )(x, ln, w_gu, w_gu, w_d)


def _sortable_key(score):
    bits = pltpu.bitcast(score, jnp.int32)
    return jnp.where(bits < 0, bits ^ jnp.int32(0x7FFFFFFF), bits)


def _kth_largest_key(key_ref, k, rows):
    def count_ge(cand):
        ge = jnp.where(key_ref[...] >= cand[None], 1.0, 0.0)
        return jnp.sum(jnp.sum(ge, axis=0), axis=-1, keepdims=True)

    zero = jnp.zeros((rows, 1), jnp.int32)
    t0 = jnp.where(count_ge(zero) >= k, zero, jnp.int32(INT32_MIN))

    def body(it, t):
        cand = t | lax.shift_left(jnp.int32(1), jnp.int32(30) - it)
        return jnp.where(count_ge(cand) >= k, cand, t)

    return lax.fori_loop(0, 31, body, t0)


def _index_scores(qi, wq, ki):
    sc = None
    for h in range(IDX_HEADS):
        d = lax.dot_general(qi[:, h * IDX_DIM:(h + 1) * IDX_DIM], ki, _NT,
                            precision=lax.Precision.HIGHEST, preferred_element_type=F32)
        t = jnp.maximum(d, 0.0) * wq[:, IDX_DIM + h:IDX_DIM + h + 1]
        sc = t if sc is None else sc + t
    return sc


def _top3_blocks(gate, own, n_blk):
    lane = lax.broadcasted_iota(jnp.int32, gate.shape, 1)
    lane_f = lane.astype(F32)
    gm = jnp.where(lane < own, gate, NEG_INF)
    gm = jnp.where(lane < n_blk, gm, -jnp.inf)
    sel = jnp.zeros(gate.shape, F32)
    for _ in range(MOBA_TOPK):
        mx = jnp.max(gm, axis=-1, keepdims=True)
        first = jnp.min(jnp.where(gm == mx, lane_f, float(LANES)), axis=-1, keepdims=True)
        pick = lane_f == first
        sel = jnp.where(pick, 1.0, sel)
        gm = jnp.where(pick, -jnp.inf, gm)
    return jnp.where(lane < own, sel, jnp.where(lane == own, 1.0, 0.0))


def _softmax_step(s, mask, vb, m, l, acc):
    s = jnp.where(mask, s, NEG_INF)
    mn = jnp.maximum(m, jnp.max(s, axis=-1, keepdims=True))
    alpha = jnp.exp(m - mn)
    p = jnp.where(mask, jnp.exp(s - mn), 0.0)
    l = alpha * l + jnp.sum(p, axis=-1, keepdims=True)
    acc = alpha * acc + jnp.dot(p.astype(BF16), vb, preferred_element_type=F32)
    return mn, l, acc


def _dsa_select_kernel(qi_ref, wq_ref, kk_ref, mask_ref, key_ref, *, topk):
    i = pl.program_id(1)
    nc, tq, kc = key_ref.shape
    qi = qi_ref[...]
    wq = wq_ref[...]
    qpos = i * tq + lax.broadcasted_iota(jnp.int32, (tq, 1), 0)
    col = lax.broadcasted_iota(jnp.int32, (tq, kc), 1)

    def chunk(c, carry):
        k0 = pl.multiple_of(c * kc, kc)
        ki = kk_ref[pl.ds(k0, kc), :][:, :IDX_DIM]
        sc = jnp.where(col + k0 <= qpos, _index_scores(qi, wq, ki), NEG_INF)
        key_ref[c] = _sortable_key(sc)
        return carry

    lax.fori_loop(0, nc, chunk, 0)
    thr = _kth_largest_key(key_ref, topk, tq)
    for c in range(nc):
        keep = (key_ref[c] >= thr) & (col + c * kc <= qpos)
        mask_ref[c] = jnp.where(keep, 1.0, 0.0).astype(BF16)


def _dsa_select_prompt(qi, kw, b, t):
    tq, kc = ATTN_Q_TILE, DSA_KEY_CHUNK
    nq, nc = t // tq, t // kc
    return pl.pallas_call(
        functools.partial(_dsa_select_kernel, topk=min(DSA_TOPK_MAX, t // 4)),
        out_shape=jax.ShapeDtypeStruct((b * nq, nc, tq, kc), BF16),
        grid=(b, nq),
        in_specs=[pl.BlockSpec((tq, IDX_Q), lambda bi, i: (bi * nq + i, 0)),
                  pl.BlockSpec((tq, LANES), lambda bi, i: (bi * nq + i, 0)),
                  pl.BlockSpec((t, LANES), lambda bi, i: (bi, 0))],
        out_specs=pl.BlockSpec((None, nc, tq, kc), lambda bi, i: (bi * nq + i, 0, 0, 0)),
        scratch_shapes=[pltpu.VMEM((nc, tq, kc), jnp.int32)],
        compiler_params=_params("parallel", "arbitrary"),
    )(qi, kw, kw)


def _prompt_attn_kernel(*refs, mode, t, kc):
    if mode == "swa":
        sink_ref, q_ref, k_ref, v_ref, o_ref = refs
    elif mode == "dsa":
        q_ref, k_ref, v_ref, mask_ref, o_ref = refs
    else:
        q_ref, k_ref, v_ref, o_ref, kmean_ref = refs
    i = pl.program_id(1)
    tq = q_ref.shape[0]
    rows = GROUP * tq
    row = lax.broadcasted_iota(jnp.int32, (rows, 1), 0)
    qpos = i * tq + _mod_pow2(row, tq)
    kcol = lax.broadcasted_iota(jnp.int32, (rows, kc), 1)

    if mode == "moba":
        n_blk = t // MOBA_BLOCK

        @pl.when(i == 0)
        def _():
            kmean_ref[...] = jnp.zeros_like(kmean_ref)
            for g in range(N_KV_HEADS):
                kg = k_ref[:, g * HEAD_DIM:(g + 1) * HEAD_DIM]
                ksum = jnp.sum(kg.reshape(n_blk, MOBA_BLOCK, HEAD_DIM), axis=1)
                kmean_ref[g, 0:n_blk, :] = ksum * (1.0 / MOBA_BLOCK)

    if mode == "swa":
        lo, hi = jnp.maximum(i - 1, 0), i + 1
    else:
        lo, hi = 0, _div_pow2((i + 1) * tq + kc - 1, kc)

    for g in range(N_KV_HEADS):
        gsl = slice(g * HEAD_DIM, (g + 1) * HEAD_DIM)
        qg = jnp.concatenate(
            [q_ref[:, (g * GROUP + n) * HEAD_DIM:(g * GROUP + n + 1) * HEAD_DIM] for n in range(GROUP)], axis=0)
        qb = qg.astype(BF16)
        if mode == "moba":
            gate = lax.dot_general(qg, kmean_ref[g], _NT, precision=lax.Precision.HIGHEST,
                                   preferred_element_type=F32)
            allow = _top3_blocks(gate, _div_pow2(qpos, MOBA_BLOCK), n_blk)
            blk_lane = lax.broadcasted_iota(jnp.int32, allow.shape, 1)

        def body(c, carry, g=g, gsl=gsl, qb=qb):
            m, l, acc = carry
            k0 = pl.multiple_of(c * kc, kc)
            kb = k_ref[pl.ds(k0, kc), gsl].astype(BF16)
            vb = v_ref[pl.ds(k0, kc), gsl].astype(BF16)
            s = lax.dot_general(qb, kb, _NT, preferred_element_type=F32) * ATTN_SCALE
            kpos = kcol + k0
            if mode == "dsa":
                mk = mask_ref[c]
                mask = jnp.concatenate([mk] * GROUP, axis=0).astype(F32) > 0.5
            elif mode == "moba":
                picked = jnp.max(jnp.where(blk_lane == c, allow, 0.0), axis=-1, keepdims=True)
                mask = (picked > 0.5) & (kpos <= qpos)
            else:
                diff = qpos - kpos
                mask = (diff >= 0) & (diff < WINDOW)
            return _softmax_step(s, mask, vb, m, l, acc)

        if mode == "swa":
            m0 = _per_head_rows(sink_ref, g, _div_pow2(row, tq))
            l0 = jnp.ones((rows, 1), F32)
        else:
            m0 = jnp.full((rows, 1), NEG_INF, F32)
            l0 = jnp.zeros((rows, 1), F32)
        m, l, acc = lax.fori_loop(lo, hi, body, (m0, l0, jnp.zeros((rows, HEAD_DIM), F32)))
        o = acc / l
        for n in range(GROUP):
            hsl = slice((g * GROUP + n) * HEAD_DIM, (g * GROUP + n + 1) * HEAD_DIM)
            o_ref[:, hsl] = o[n * tq:(n + 1) * tq]


def _prompt_attn(qkv, b, t, mode, mask=None, sinks=None):
    tq = ATTN_Q_TILE
    nq = t // tq
    kc = {"dsa": DSA_KEY_CHUNK, "moba": MOBA_BLOCK, "swa": WINDOW}[mode]
    in_specs = [pl.BlockSpec((tq, Q_DIM), lambda bi, i: (bi * nq + i, 0)),
                pl.BlockSpec((t, KV_DIM), lambda bi, i: (bi, Q_DIM // KV_DIM)),
                pl.BlockSpec((t, KV_DIM), lambda bi, i: (bi, Q_DIM // KV_DIM + 1))]
    args = [qkv, qkv, qkv]
    scratch = []
    if mode == "dsa":
        nc = t // kc
        in_specs.append(pl.BlockSpec((None, nc, tq, kc), lambda bi, i: (bi * nq + i, 0, 0, 0)))
        args.append(mask)
    elif mode == "moba":
        assert t % MOBA_BLOCK == 0 and t // MOBA_BLOCK <= LANES
        scratch.append(pltpu.VMEM((N_KV_HEADS, LANES, HEAD_DIM), F32))
    else:
        assert tq == WINDOW
        in_specs.insert(0, pl.BlockSpec(memory_space=pltpu.SMEM))
        args.insert(0, sinks)
    return pl.pallas_call(
        functools.partial(_prompt_attn_kernel, mode=mode, t=t, kc=kc),
        out_shape=jax.ShapeDtypeStruct((b * t, Q_DIM), F32),
        grid=(b, nq),
        in_specs=in_specs,
        out_specs=pl.BlockSpec((tq, Q_DIM), lambda bi, i: (bi * nq + i, 0)),
        scratch_shapes=scratch,
        compiler_params=_params("parallel", "arbitrary"),
    )(*args)


def _page_specs(shape_tail, n_pages, pages_per_step):
    def spec(j):
        def index(bi, c, pt):
            page = jnp.minimum(c * pages_per_step + j, n_pages - 1)
            return (pt[bi * n_pages + page],) + (0,) * len(shape_tail)
        return pl.BlockSpec((None,) + shape_tail, index)
    return [spec(j) for j in range(pages_per_step)]


def _dsa_sample_select_kernel(pt_ref, qi_ref, wq_ref, *refs, pages, topk, s_len):
    ik_refs = refs[:pages]
    knew_ref, mask_ref, key_ref = refs[pages:]
    c = pl.program_id(1)
    last = pl.num_programs(1) - 1
    nsteps, rows, width = key_ref.shape
    qi = qi_ref[...]
    wq = wq_ref[...]
    tok = _mod_pow2(lax.broadcasted_iota(jnp.int32, (rows, width), 0), s_len)
    new_ok = lax.broadcasted_iota(jnp.int32, (rows, width), 1) <= tok

    @pl.when(c < last)
    def _():
        ki = jnp.concatenate([r[...] for r in ik_refs], axis=0)
        key_ref[c] = _sortable_key(_index_scores(qi, wq, ki))

    @pl.when(c == last)
    def _():
        sc = _index_scores(qi, wq, knew_ref[...])
        sc = jnp.concatenate([sc, jnp.zeros((rows, width - PAGE_SIZE), F32)], axis=1)
        key_ref[c] = _sortable_key(jnp.where(new_ok, sc, NEG_INF))
        thr = _kth_largest_key(key_ref, topk, rows)
        for cc in range(nsteps):
            keep = key_ref[cc] >= thr
            if cc == nsteps - 1:
                keep = keep & new_ok
            mask_ref[0, :, cc * width:(cc + 1) * width] = jnp.where(keep, 1.0, 0.0).astype(BF16)


def _dsa_select_sample(pt_flat, qi16, wq16, cache_ik, ki_new, n_pages, s_len):
    db, rows, _ = qi16.shape
    pages = min(PAGES_PER_STEP, n_pages)
    assert n_pages % pages == 0
    nsteps = n_pages // pages + 1
    width = pages * PAGE_SIZE
    lp = nsteps * width
    bsel = lambda bi, c, pt: (bi, 0, 0)
    grid_spec = pltpu.PrefetchScalarGridSpec(
        num_scalar_prefetch=1,
        grid=(db, nsteps),
        in_specs=[pl.BlockSpec((None, rows, IDX_Q), bsel), pl.BlockSpec((None, rows, LANES), bsel)]
        + _page_specs((PAGE_SIZE, IDX_DIM), n_pages, pages)
        + [pl.BlockSpec((None, PAGE_SIZE, IDX_DIM), bsel)],
        out_specs=pl.BlockSpec((None, 1, rows, lp), lambda bi, c, pt: (bi, 0, 0, 0)),
        scratch_shapes=[pltpu.VMEM((nsteps, rows, width), jnp.int32)],
    )
    topk = min(DSA_TOPK_MAX, (n_pages * PAGE_SIZE + s_len) // 4)
    return pl.pallas_call(
        functools.partial(_dsa_sample_select_kernel, pages=pages, topk=topk, s_len=s_len),
        out_shape=jax.ShapeDtypeStruct((db, 1, rows, lp), BF16),
        grid_spec=grid_spec,
        compiler_params=_params("parallel", "arbitrary"),
    )(pt_flat, qi16, wq16, *([cache_ik] * pages), ki_new)


def _moba_block_mean_kernel(pt_ref, *refs, pages):
    k_refs = refs[:pages]
    o_ref = refs[pages]
    per_blk = MOBA_BLOCK // PAGE_SIZE
    for j in range(pages // per_blk):
        blk = jnp.concatenate([k_refs[j * per_blk + u][...] for u in range(per_blk)], axis=0)
        o_ref[j:j + 1, :] = jnp.sum(blk, axis=0, keepdims=True) * (1.0 / MOBA_BLOCK)


def _moba_block_means(pt_flat, cache_k, db, n_pages):
    pages = MOBA_SUM_PAGES
    assert n_pages % pages == 0
    per_step = pages * PAGE_SIZE // MOBA_BLOCK
    grid_spec = pltpu.PrefetchScalarGridSpec(
        num_scalar_prefetch=1,
        grid=(db, n_pages // pages),
        in_specs=_page_specs((PAGE_SIZE, KV_DIM), n_pages, pages),
        out_specs=pl.BlockSpec((None, per_step, KV_DIM), lambda bi, c, pt: (bi, c, 0)),
    )
    return pl.pallas_call(
        functools.partial(_moba_block_mean_kernel, pages=pages),
        out_shape=jax.ShapeDtypeStruct((db, n_pages * PAGE_SIZE // MOBA_BLOCK, KV_DIM), F32),
        grid_spec=grid_spec,
        compiler_params=_params("parallel", "arbitrary"),
    )(pt_flat, *([cache_k] * pages))


def _moba_sample_gate_kernel(q_ref, kmean_ref, allow_ref, *, n_past_blk, n_blk, s_len, past_len):
    rows = q_ref.shape[1]
    tok = _mod_pow2(lax.broadcasted_iota(jnp.int32, (rows, 1), 0), s_len)
    own = _div_pow2(past_len + tok, MOBA_BLOCK)
    for g in range(N_KV_HEADS):
        km = kmean_ref[:, g * HEAD_DIM:(g + 1) * HEAD_DIM]
        km = jnp.concatenate([km, jnp.zeros((LANES - n_past_blk, HEAD_DIM), F32)], axis=0)
        gate = lax.dot_general(q_ref[g], km, _NT, precision=lax.Precision.HIGHEST, preferred_element_type=F32)
        allow_ref[g] = _top3_blocks(gate, own, n_blk)


def _moba_sample_gate(qs, kmean, n_pages, s_len):
    db, _, rows, _ = qs.shape
    past_len = n_pages * PAGE_SIZE
    n_past_blk = kmean.shape[1]
    n_blk = -(-(past_len + s_len) // MOBA_BLOCK)
    assert past_len % MOBA_BLOCK == 0 and s_len <= MOBA_BLOCK and n_blk <= LANES
    return pl.pallas_call(
        functools.partial(_moba_sample_gate_kernel, n_past_blk=n_past_blk, n_blk=n_blk, s_len=s_len,
                          past_len=past_len),
        out_shape=jax.ShapeDtypeStruct((db, N_KV_HEADS, rows, LANES), F32),
        grid=(db,),
        in_specs=[pl.BlockSpec((None, N_KV_HEADS, rows, HEAD_DIM), lambda bi: (bi, 0, 0, 0)),
                  pl.BlockSpec((None, n_past_blk, KV_DIM), lambda bi: (bi, 0, 0))],
        out_specs=pl.BlockSpec((None, N_KV_HEADS, rows, LANES), lambda bi: (bi, 0, 0, 0)),
        compiler_params=_params("parallel"),
    )(qs, kmean)


def _paged_attn_kernel(pt_ref, *refs, pages, mode, use_sink, s_len, past_len):
    if use_sink:
        sink_ref, refs = refs[0], refs[1:]
    q_ref = refs[0]
    k_refs = refs[1:1 + pages]
    v_refs = refs[1 + pages:1 + 2 * pages]
    knew_ref, vnew_ref, sel_ref, o_ref, m_ref, l_ref, acc_ref = refs[1 + 2 * pages:]
    c = pl.program_id(1)
    last = pl.num_programs(1) - 1
    rows = q_ref.shape[1]
    width = pages * PAGE_SIZE
    row = lax.broadcasted_iota(jnp.int32, (rows, 1), 0)
    tok = _mod_pow2(row, s_len)

    @pl.when(c == 0)
    def _():
        for g in range(N_KV_HEADS):
            if use_sink:
                m_ref[g] = _per_head_rows(sink_ref, g, _div_pow2(row, s_len))
                l_ref[g] = jnp.ones((rows, 1), F32)
            else:
                m_ref[g] = jnp.full((rows, 1), NEG_INF, F32)
                l_ref[g] = jnp.zeros((rows, 1), F32)
            acc_ref[g] = jnp.zeros((rows, HEAD_DIM), F32)

    def block_allowed(g, blk):
        lane = lax.broadcasted_iota(jnp.int32, (rows, LANES), 1)
        return jnp.max(jnp.where(lane == blk, sel_ref[g], 0.0), axis=-1, keepdims=True) > 0.5

    def update(g, kg, vg, mask):
        s = lax.dot_general(q_ref[g].astype(BF16), kg.astype(BF16), _NT, preferred_element_type=F32) * ATTN_SCALE
        m, l, acc = _softmax_step(s, mask, vg.astype(BF16), m_ref[g], l_ref[g], acc_ref[g])
        m_ref[g] = m
        l_ref[g] = l
        acc_ref[g] = acc

    @pl.when(c < last)
    def _():
        for g in range(N_KV_HEADS):
            gsl = slice(g * HEAD_DIM, (g + 1) * HEAD_DIM)
            kg = jnp.concatenate([r[:, gsl] for r in k_refs], axis=0)
            vg = jnp.concatenate([r[:, gsl] for r in v_refs], axis=0)
            if mode == "mask":
                mask = sel_ref[0].astype(F32) > 0.5
            else:
                mask = jnp.concatenate(
                    [jnp.broadcast_to(block_allowed(g, _div_pow2(c * pages + j, MOBA_BLOCK // PAGE_SIZE)),
                                      (rows, PAGE_SIZE)) for j in range(pages)], axis=1)
            update(g, kg, vg, mask)

    @pl.when(c == last)
    def _():
        for g in range(N_KV_HEADS):
            gsl = slice(g * HEAD_DIM, (g + 1) * HEAD_DIM)
            if mode == "mask":
                mask = sel_ref[0][:, :PAGE_SIZE].astype(F32) > 0.5
            else:
                col = lax.broadcasted_iota(jnp.int32, (rows, PAGE_SIZE), 1)
                mask = block_allowed(g, past_len // MOBA_BLOCK) & (col <= tok)
            update(g, knew_ref[:, gsl], vnew_ref[:, gsl], mask)
            o_ref[g] = acc_ref[g] / l_ref[g]


def _paged_attn(pt_flat, qs, cache_k, cache_v, k_new, v_new, sel, n_pages, s_len, mode, sinks=None):
    db, _, rows, _ = qs.shape
    pages = min(PAGES_PER_STEP, n_pages)
    assert n_pages % pages == 0
    nsteps = n_pages // pages + 1
    width = pages * PAGE_SIZE
    bsel3 = lambda bi, c, pt: (bi, 0, 0)
    bsel4 = lambda bi, c, pt: (bi, 0, 0, 0)
    if mode == "mask":
        sel_spec = pl.BlockSpec((None, 1, rows, width), lambda bi, c, pt: (bi, 0, 0, c))
    else:
        assert (n_pages * PAGE_SIZE) % MOBA_BLOCK == 0 and s_len <= MOBA_BLOCK
        sel_spec = pl.BlockSpec((None, N_KV_HEADS, rows, LANES), bsel4)
    in_specs = ([pl.BlockSpec((None, N_KV_HEADS, rows, HEAD_DIM), bsel4)]
                + _page_specs((PAGE_SIZE, KV_DIM), n_pages, pages) * 2
                + [pl.BlockSpec((None, PAGE_SIZE, KV_DIM), bsel3)] * 2 + [sel_spec])
    args = [qs] + [cache_k] * pages + [cache_v] * pages + [k_new, v_new, sel]
    if sinks is not None:
        in_specs.insert(0, pl.BlockSpec(memory_space=pltpu.SMEM))
        args.insert(0, sinks)
    grid_spec = pltpu.PrefetchScalarGridSpec(
        num_scalar_prefetch=1,
        grid=(db, nsteps),
        in_specs=in_specs,
        out_specs=pl.BlockSpec((None, N_KV_HEADS, rows, HEAD_DIM), bsel4),
        scratch_shapes=[pltpu.VMEM((N_KV_HEADS, rows, 1), F32), pltpu.VMEM((N_KV_HEADS, rows, 1), F32),
                        pltpu.VMEM((N_KV_HEADS, rows, HEAD_DIM), F32)],
    )
    return pl.pallas_call(
        functools.partial(_paged_attn_kernel, pages=pages, mode=mode, use_sink=sinks is not None,
                          s_len=s_len, past_len=n_pages * PAGE_SIZE),
        out_shape=jax.ShapeDtypeStruct((db, N_KV_HEADS, rows, HEAD_DIM), F32),
        grid_spec=grid_spec,
        compiler_params=_params("parallel", "arbitrary"),
    )(pt_flat, *args)


def _pad_rows(a, rows):
    return jnp.pad(a, ((0, 0), (0, rows - a.shape[1]), (0, 0)))


def kernel(x_prompt, x_sample, cache_k_0, cache_v_0, cache_ik_0, cache_k_1, cache_v_1, state_wk_2, state_wv_2,
           cache_k_3, cache_v_3, cache_ik_3, page_table, ln_attn, w_qkv, q_norm, k_norm, w_o, w_idx, idx_ln_g,
           idx_ln_b, sinks, ln_ffn, w_gate_up, w_down):
    b, t, d = x_prompt.shape
    db, s_len, _ = x_sample.shape
    n_pages = page_table.shape[1]
    past_len = n_pages * PAGE_SIZE
    depth = w_qkv.shape[0]
    n_p = b * t
    n = n_p + db * s_len
    tm = _row_tile(n)
    rows = GROUP * s_len
    assert t % DSA_KEY_CHUNK == 0 and t % ATTN_Q_TILE == 0 and rows % 8 == 0

    caches = {0: (cache_k_0, cache_v_0, cache_ik_0), 1: (cache_k_1, cache_v_1), 2: (state_wk_2, state_wv_2),
              3: (cache_k_3, cache_v_3, cache_ik_3)}
    pt_flat = page_table.reshape(-1).astype(jnp.int32)
    win_pt = jnp.arange(db, dtype=jnp.int32)

    pos = jnp.concatenate([jnp.tile(jnp.arange(t, dtype=jnp.int32), b),
                           jnp.tile(past_len + jnp.arange(s_len, dtype=jnp.int32), db)])
    tabs_main = _rope_tables(pos, ROPE_DIM // 2, HEAD_DIM)
    tabs_iq = _rope_tables(pos, IDX_ROPE_DIM // 2, IDX_DIM)
    tabs_ik = _rope_tables(pos, IDX_ROPE_DIM // 2, LANES)

    tok = (jnp.arange(rows, dtype=jnp.int32) % s_len)[:, None]
    col = jnp.arange(PAGE_SIZE, dtype=jnp.int32)[None, :]
    swa_mask = jnp.concatenate([col > tok + (WINDOW - PAGE_SIZE), col <= tok], axis=1)
    swa_mask = jnp.broadcast_to(swa_mask.astype(BF16)[None, None], (db, 1, rows, 2 * PAGE_SIZE))

    x = jnp.concatenate([x_prompt.reshape(n_p, d), x_sample.reshape(db * s_len, d)], axis=0)
    outs = {}
    for i in range(depth):
        kind, j = i % 3, i // 3
        gain = jnp.concatenate([jnp.tile(q_norm[i], N_HEADS), jnp.tile(k_norm[i], N_KV_HEADS),
                                jnp.ones((KV_DIM,), F32)])[None, :]
        ln_a = ln_attn[i][None, :]
        qkv = _qkv_proj(x, ln_a, w_qkv[i].astype(BF16), gain, tabs_main, tm)
        k_all = qkv[:, Q_DIM:Q_DIM + KV_DIM]
        v_all = qkv[:, Q_DIM + KV_DIM:]
        k_p = k_all[:n_p].reshape(b, t, N_KV_HEADS, HEAD_DIM)
        v_p = v_all[:n_p].reshape(b, t, N_KV_HEADS, HEAD_DIM)
        k_s = k_all[n_p:].reshape(db, s_len, N_KV_HEADS, HEAD_DIM)
        v_s = v_all[n_p:].reshape(db, s_len, N_KV_HEADS, HEAD_DIM)
        k_new = _pad_rows(k_all[n_p:].reshape(db, s_len, KV_DIM), PAGE_SIZE)
        v_new = _pad_rows(v_all[n_p:].reshape(db, s_len, KV_DIM), PAGE_SIZE)
        qs = qkv[n_p:, :Q_DIM].reshape(db, s_len, N_KV_HEADS, GROUP, HEAD_DIM)
        qs = qs.transpose(0, 2, 3, 1, 4).reshape(db, N_KV_HEADS, rows, HEAD_DIM)

        if kind == 0:
            ck, cv, cik = caches[i]
            w_i = jnp.pad(w_idx[j], ((0, 0), (0, IDX_PROJ_PAD - IDX_PROJ))).astype(BF16)
            lg = jnp.pad(idx_ln_g[j], (0, LANES - IDX_DIM))[None, :]
            lb = jnp.pad(idx_ln_b[j], (0, LANES - IDX_DIM))[None, :]
            qi, kw = _idx_proj(x, ln_a, w_i, lg, lb, tabs_iq, tabs_ik, tm)
            mask_p = _dsa_select_prompt(qi, kw, b, t)
            o_p = _prompt_attn(qkv, b, t, "dsa", mask=mask_p)
            qi16 = jnp.tile(qi[n_p:].reshape(db, s_len, IDX_Q), (1, GROUP, 1))
            wq16 = jnp.tile(kw[n_p:].reshape(db, s_len, LANES), (1, GROUP, 1))
            ki_new = _pad_rows(kw[n_p:, :IDX_DIM].reshape(db, s_len, IDX_DIM), PAGE_SIZE)
            mask_s = _dsa_select_sample(pt_flat, qi16, wq16, cik, ki_new, n_pages, s_len)
            o_s = _paged_attn(pt_flat, qs, ck.reshape(-1, PAGE_SIZE, KV_DIM), cv.reshape(-1, PAGE_SIZE, KV_DIM),
                              k_new, v_new, mask_s, n_pages, s_len, "mask")
            ik_p = kw[:n_p, :IDX_DIM].reshape(b, t, IDX_DIM)
            ik_s = kw[n_p:, :IDX_DIM].reshape(db, s_len, IDX_DIM)
            outs[i] = (k_p, v_p, ik_p, k_s, v_s, ik_s)
        elif kind == 1:
            ck, cv = caches[i]
            ck2 = ck.reshape(-1, PAGE_SIZE, KV_DIM)
            cv2 = cv.reshape(-1, PAGE_SIZE, KV_DIM)
            o_p = _prompt_attn(qkv, b, t, "moba")
            kmean = _moba_block_means(pt_flat, ck2, db, n_pages)
            allow = _moba_sample_gate(qs, kmean, n_pages, s_len)
            o_s = _paged_attn(pt_flat, qs, ck2, cv2, k_new, v_new, allow, n_pages, s_len, "block")
            outs[i] = (k_p, v_p, k_s, v_s)
        else:
            bk, bv = caches[i]
            o_p = _prompt_attn(qkv, b, t, "swa", sinks=sinks[j])
            o_s = _paged_attn(win_pt, qs, bk.reshape(db, WINDOW, KV_DIM), bv.reshape(db, WINDOW, KV_DIM),
                              k_new, v_new, swa_mask, 1, s_len, "mask", sinks=sinks[j])
            outs[i] = (k_p[:, -WINDOW:], v_p[:, -WINDOW:],
                       jnp.concatenate([bk, k_s], axis=1)[:, -WINDOW:],
                       jnp.concatenate([bv, v_s], axis=1)[:, -WINDOW:])

        o_s = o_s.reshape(db, N_KV_HEADS, GROUP, s_len, HEAD_DIM).transpose(0, 3, 1, 2, 4).reshape(db * s_len, Q_DIM)
        o = jnp.concatenate([o_p, o_s], axis=0)
        x = _out_proj(o, w_o[i].astype(BF16), x, tm)
        x = _ffn(x, ln_ffn[i][None, :], w_gate_up[i].astype(BF16), w_down[i].astype(BF16), tm)

    y_p = x[:n_p].reshape(b, t, d)
    y_s = x[n_p:].reshape(db, s_len, d)
    return (y_p, y_s) + outs[0] + outs[1] + outs[2] + outs[3]
```

```python
import functools

import jax
import jax.numpy as jnp
from jax import lax
from jax.experimental import pallas as pl
from jax.experimental.pallas import tpu as pltpu

F32 = jnp.float32
BF16 = jnp.bfloat16

N_HEADS = 16
N_KV_HEADS = 4
HEAD_DIM = 128
GROUP = N_HEADS // N_KV_HEADS
Q_DIM = N_HEADS * HEAD_DIM
KV_DIM = N_KV_HEADS * HEAD_DIM
QKV_DIM = Q_DIM + 2 * KV_DIM
ROPE_THETA = 500000.0
ROPE_DIM = HEAD_DIM // 4
IDX_HEADS = 16
IDX_DIM = 64
IDX_ROPE_DIM = IDX_DIM // 4
IDX_Q = IDX_HEADS * IDX_DIM
IDX_PROJ = IDX_Q + IDX_DIM + IDX_HEADS
IDX_PROJ_PAD = IDX_Q + 128
DSA_TOPK_MAX = 256
MOBA_BLOCK = 256
MOBA_TOPK = 3
WINDOW = 128
PAGE_SIZE = 128
RMS_EPS = 1e-6
NEG_INF = -1e30
ATTN_SCALE = HEAD_DIM ** -0.5
INT32_MIN = -(2 ** 31)

LANES = 128
VMEM_LIMIT = 56 * 1024 * 1024
ROW_TILE_MAX = 640
QKV_COL_TILE = 512
FFN_COL_TILE = 512
ATTN_Q_TILE = 128
DSA_KEY_CHUNK = 512
PAGES_PER_STEP = 16
MOBA_SUM_PAGES = 16

_NT = (((1,), (1,)), ((), ()))


def _params(*sem):
    return pltpu.CompilerParams(dimension_semantics=sem, vmem_limit_bytes=VMEM_LIMIT)


def _log2(v):
    assert v > 0 and v & (v - 1) == 0, v
    return v.bit_length() - 1


def _div_pow2(x, d):
    return x >> _log2(d)


def _mod_pow2(x, d):
    _log2(d)
    return x & (d - 1)


def _per_head_rows(sink_ref, g, row_head):
    out = jnp.full(row_head.shape, sink_ref[g * GROUP + GROUP - 1], F32)
    for n in range(GROUP - 1):
        out = jnp.where(row_head == n, sink_ref[g * GROUP + n], out)
    return out


def _row_tile(n):
    best = None
    for t in range(LANES, ROW_TILE_MAX + 1, LANES):
        if n % t == 0:
            best = t
    assert best is not None, n
    return best


def _rope_tables(pos, half, period):
    inv_freq = ROPE_THETA ** (-jnp.arange(half, dtype=F32) / half)
    ang = pos.astype(F32)[:, None] * inv_freq[None, :]
    cos = jnp.cos(ang)
    sin = jnp.sin(ang)
    r = jnp.arange(LANES) % period
    lo = r < half
    hi = (r >= half) & (r < 2 * half)
    idx = jnp.clip(jnp.where(lo, r, r - half), 0, half - 1)
    cg = cos[:, idx]
    sg = sin[:, idx]
    return (jnp.where(lo | hi, cg, 1.0).astype(F32),
            jnp.where(lo, -sg, 0.0).astype(F32),
            jnp.where(hi, sg, 0.0).astype(F32))


def _rope(y, cos, sin_a, sin_b, half):
    return y * cos + pltpu.roll(y, LANES - half, 1) * sin_a + pltpu.roll(y, half, 1) * sin_b


def _rms_rows_bf16(x_ref, ln_ref):
    xf = x_ref[...]
    ms = jnp.mean(xf * xf, axis=-1, keepdims=True)
    return (xf * lax.rsqrt(ms + RMS_EPS) * ln_ref[...]).astype(BF16)


def _qkv_kernel(x_ref, ln_ref, w_ref, g_ref, cos_ref, sa_ref, sb_ref, o_ref, h_ref, *, n_norm_tiles):
    j = pl.program_id(1)

    @pl.when(j == 0)
    def _():
        h_ref[...] = _rms_rows_bf16(x_ref, ln_ref)

    acc = jnp.dot(h_ref[...], w_ref[...], preferred_element_type=F32)

    @pl.when(j < n_norm_tiles)
    def _():
        cos = cos_ref[...]
        sa = sa_ref[...]
        sb = sb_ref[...]
        for hh in range(acc.shape[1] // HEAD_DIM):
            sl = slice(hh * HEAD_DIM, (hh + 1) * HEAD_DIM)
            a = acc[:, sl]
            ms = jnp.mean(a * a, axis=-1, keepdims=True)
            y = a * lax.rsqrt(ms + RMS_EPS) * g_ref[:, sl]
            o_ref[:, sl] = _rope(y, cos, sa, sb, ROPE_DIM // 2)

    @pl.when(j >= n_norm_tiles)
    def _():
        o_ref[...] = acc


def _qkv_proj(x, ln, w, gain, tabs, tm):
    n, d = x.shape
    tn = QKV_COL_TILE
    tab_spec = pl.BlockSpec((tm, LANES), lambda i, j: (i, 0))
    return pl.pallas_call(
        functools.partial(_qkv_kernel, n_norm_tiles=(Q_DIM + KV_DIM) // tn),
        out_shape=jax.ShapeDtypeStruct((n, QKV_DIM), F32),
        grid=(n // tm, QKV_DIM // tn),
        in_specs=[pl.BlockSpec((tm, d), lambda i, j: (i, 0)),
                  pl.BlockSpec((1, d), lambda i, j: (0, 0)),
                  pl.BlockSpec((d, tn), lambda i, j: (0, j)),
                  pl.BlockSpec((1, tn), lambda i, j: (0, j)),
                  tab_spec, tab_spec, tab_spec],
        out_specs=pl.BlockSpec((tm, tn), lambda i, j: (i, j)),
        scratch_shapes=[pltpu.VMEM((tm, d), BF16)],
        compiler_params=_params("parallel", "arbitrary"),
    )(x, ln, w, gain, *tabs)


def _idx_kernel(x_ref, ln_ref, w_ref, lg_ref, lb_ref, cq_ref, saq_ref, sbq_ref,
                ck_ref, sak_ref, sbk_ref, qi_ref, kw_ref):
    h = _rms_rows_bf16(x_ref, ln_ref)
    acc = jnp.dot(h, w_ref[...], preferred_element_type=F32)
    half = IDX_ROPE_DIM // 2
    cq = cq_ref[...]
    saq = saq_ref[...]
    sbq = sbq_ref[...]
    for c in range(IDX_Q // LANES):
        sl = slice(c * LANES, (c + 1) * LANES)
        qi_ref[:, sl] = _rope(acc[:, sl], cq, saq, sbq, half)
    a = acc[:, IDX_Q:IDX_Q + LANES]
    lane = lax.broadcasted_iota(jnp.int32, a.shape, 1)
    is_key = lane < IDX_DIM
    mu = jnp.sum(jnp.where(is_key, a, 0.0), axis=-1, keepdims=True) * (1.0 / IDX_DIM)
    xc = a - mu
    var = jnp.sum(jnp.where(is_key, xc * xc, 0.0), axis=-1, keepdims=True) * (1.0 / IDX_DIM)
    y = xc * lax.rsqrt(var + RMS_EPS) * lg_ref[...] + lb_ref[...]
    kn = jnp.where(is_key, y, a * (IDX_HEADS ** -0.5 * IDX_DIM ** -0.5))
    kw_ref[...] = _rope(kn, ck_ref[...], sak_ref[...], sbk_ref[...], half)


def _idx_proj(x, ln, w, lg, lb, tabs_q, tabs_k, tm):
    n, d = x.shape
    row = lambda i: (i, 0)
    fix = lambda i: (0, 0)
    tab_spec = pl.BlockSpec((tm, LANES), row)
    return pl.pallas_call(
        _idx_kernel,
        out_shape=(jax.ShapeDtypeStruct((n, IDX_Q), F32), jax.ShapeDtypeStruct((n, LANES), F32)),
        grid=(n // tm,),
        in_specs=[pl.BlockSpec((tm, d), row), pl.BlockSpec((1, d), fix),
                  pl.BlockSpec((d, IDX_PROJ_PAD), fix),
                  pl.BlockSpec((1, LANES), fix), pl.BlockSpec((1, LANES), fix)] + [tab_spec] * 6,
        out_specs=(pl.BlockSpec((tm, IDX_Q), row), pl.BlockSpec((tm, LANES), row)),
        compiler_params=_params("parallel"),
    )(x, ln, w, lg, lb, *tabs_q, *tabs_k)


def _oproj_kernel(o_ref, w_ref, x_ref, out_ref, ob_ref):
    @pl.when(pl.program_id(1) == 0)
    def _():
        ob_ref[...] = o_ref[...].astype(BF16)

    out_ref[...] = x_ref[...] + jnp.dot(ob_ref[...], w_ref[...], preferred_element_type=F32)


def _out_proj(o, w, x, tm):
    n, d = x.shape
    tn = min(QKV_COL_TILE, d)
    return pl.pallas_call(
        _oproj_kernel,
        out_shape=jax.ShapeDtypeStruct((n, d), F32),
        grid=(n // tm, d // tn),
        in_specs=[pl.BlockSpec((tm, Q_DIM), lambda i, j: (i, 0)),
                  pl.BlockSpec((Q_DIM, tn), lambda i, j: (0, j)),
                  pl.BlockSpec((tm, tn), lambda i, j: (i, j))],
        out_specs=pl.BlockSpec((tm, tn), lambda i, j: (i, j)),
        scratch_shapes=[pltpu.VMEM((tm, Q_DIM), BF16)],
        compiler_params=_params("parallel", "arbitrary"),
    )(o, w, x)


def _ffn_kernel(x_ref, ln_ref, wg_ref, wu_ref, wd_ref, o_ref, h_ref, acc_ref):
    f = pl.program_id(1)

    @pl.when(f == 0)
    def _():
        h_ref[...] = _rms_rows_bf16(x_ref, ln_ref)
        acc_ref[...] = jnp.zeros_like(acc_ref)

    h = h_ref[...]
    g = jnp.dot(h, wg_ref[...], preferred_element_type=F32)
    u = jnp.dot(h, wu_ref[...], preferred_element_type=F32)
    a = (g * jax.nn.sigmoid(g)) * u
    acc_ref[...] += jnp.dot(a.astype(BF16), wd_ref[...], preferred_element_type=F32)

    @pl.when(f == pl.num_programs(1) - 1)
    def _():
        o_ref[...] = x_ref[...] + acc_ref[...]


def _ffn(x, ln, w_gu, w_d, tm):
    n, d = x.shape
    d_ff = w_d.shape[0]
    tf = min(FFN_COL_TILE, d_ff)
    nf = d_ff // tf
    return pl.pallas_call(
        _ffn_kernel,
        out_shape=jax.ShapeDtypeStruct((n, d), F32),
        grid=(n // tm, nf),
        in_specs=[pl.BlockSpec((tm, d), lambda i, f: (i, 0)),
                  pl.BlockSpec((1, d), lambda i, f: (0, 0)),
                  pl.BlockSpec((d, tf), lambda i, f: (0, f)),
                  pl.BlockSpec((d, tf), lambda i, f: (0, f + nf)),
                  pl.BlockSpec((tf, d), lambda i, f: (f, 0))],
        out_specs=pl.BlockSpec((tm, d), lambda i, f: (i, 0)),
        scratch_shapes=[pltpu.VMEM((tm, d), BF16), pltpu.VMEM((tm, d), F32)],
        compiler_params=_params("parallel", "arbitrary"),
    )(x, ln, w_gu, w_gu, w_d)


def _sortable_key(score):
    bits = pltpu.bitcast(score, jnp.int32)
    return jnp.where(bits < 0, bits ^ jnp.int32(0x7FFFFFFF), bits)


def _kth_largest_key(key_ref, k, rows, n_chunks=None):
    width = key_ref.shape[2]

    def count_ge(cand):
        if n_chunks is None:
            ge = jnp.where(key_ref[...] >= cand[None], 1.0, 0.0)
            return jnp.sum(jnp.sum(ge, axis=0), axis=-1, keepdims=True)

        def add_chunk(c, part):
            ge = jnp.where(key_ref[c] >= cand, 1.0, 0.0)
            for w in range(width // LANES):
                part = part + ge[:, w * LANES:(w + 1) * LANES]
            return part

        part = lax.fori_loop(0, n_chunks, add_chunk, jnp.zeros((rows, LANES), F32))
        return jnp.sum(part, axis=-1, keepdims=True)

    zero = jnp.zeros((rows, 1), jnp.int32)
    t0 = jnp.where(count_ge(zero) >= k, zero, jnp.int32(INT32_MIN))

    def body(it, t):
        cand = t | lax.shift_left(jnp.int32(1), jnp.int32(30) - it)
        return jnp.where(count_ge(cand) >= k, cand, t)

    return lax.fori_loop(0, 31, body, t0)


def _index_scores(qi, wq, ki):
    sc = None
    for h in range(IDX_HEADS):
        qh = qi[:, h * IDX_DIM:(h + 1) * IDX_DIM].astype(BF16)
        d = lax.dot_general(qh, ki, _NT, preferred_element_type=F32)
        t = jnp.maximum(d, 0.0) * wq[:, IDX_DIM + h:IDX_DIM + h + 1]
        sc = t if sc is None else sc + t
    return sc


def _top3_blocks(gate, own, n_blk):
    lane = lax.broadcasted_iota(jnp.int32, gate.shape, 1)
    lane_f = lane.astype(F32)
    gm = jnp.where(lane < own, gate, NEG_INF)
    gm = jnp.where(lane < n_blk, gm, -jnp.inf)
    sel = jnp.zeros(gate.shape, F32)
    for _ in range(MOBA_TOPK):
        mx = jnp.max(gm, axis=-1, keepdims=True)
        first = jnp.min(jnp.where(gm == mx, lane_f, float(LANES)), axis=-1, keepdims=True)
        pick = lane_f == first
        sel = jnp.where(pick, 1.0, sel)
        gm = jnp.where(pick, -jnp.inf, gm)
    return jnp.where(lane < own, sel, jnp.where(lane == own, 1.0, 0.0))


def _softmax_probs(s, mask, m, l):
    s = jnp.where(mask, s, NEG_INF)
    mn = jnp.maximum(m, jnp.max(s, axis=-1, keepdims=True))
    alpha = jnp.exp(m - mn)
    p = jnp.exp(s - mn)
    return p, mn, alpha * l + jnp.sum(p, axis=-1, keepdims=True), alpha


def _softmax_step(s, mask, vb, m, l, acc):
    p, mn, l, alpha = _softmax_probs(s, mask, m, l)
    acc = alpha * acc + jnp.dot(p.astype(BF16), vb, preferred_element_type=F32)
    return mn, l, acc


def _dsa_select_kernel(qi_ref, wq_ref, kk_ref, mask_ref, key_ref, *, topk):
    i = pl.program_id(1)
    nc, tq, kc = key_ref.shape
    qi = qi_ref[...]
    wq = wq_ref[...]
    qpos = i * tq + lax.broadcasted_iota(jnp.int32, (tq, 1), 0)
    col = lax.broadcasted_iota(jnp.int32, (tq, kc), 1)

    n_live = _div_pow2((i + 1) * tq + kc - 1, kc)

    def chunk(c, carry):
        k0 = pl.multiple_of(c * kc, kc)
        ki = kk_ref[pl.ds(k0, kc), :][:, :IDX_DIM].astype(BF16)
        sc = jnp.where(col + k0 <= qpos, _index_scores(qi, wq, ki), NEG_INF)
        key_ref[c] = _sortable_key(sc)
        return carry

    lax.fori_loop(0, n_live, chunk, 0)
    thr = _kth_largest_key(key_ref, topk, tq, n_chunks=n_live)

    def emit(c, carry):
        keep = (key_ref[c] >= thr) & (col + c * kc <= qpos)
        mask_ref[c] = jnp.where(keep, 1.0, 0.0).astype(BF16)
        return carry

    lax.fori_loop(0, n_live, emit, 0)

    def blank(c, carry):
        mask_ref[c] = jnp.zeros((tq, kc), BF16)
        return carry

    lax.fori_loop(n_live, nc, blank, 0)


def _dsa_select_prompt(qi, kw, b, t):
    tq, kc = ATTN_Q_TILE, DSA_KEY_CHUNK
    nq, nc = t // tq, t // kc
    topk = min(DSA_TOPK_MAX, t // 4)
    assert kc >= topk
    return pl.pallas_call(
        functools.partial(_dsa_select_kernel, topk=topk),
        out_shape=jax.ShapeDtypeStruct((b * nq, nc, tq, kc), BF16),
        grid=(b, nq),
        in_specs=[pl.BlockSpec((tq, IDX_Q), lambda bi, i: (bi * nq + i, 0)),
                  pl.BlockSpec((tq, LANES), lambda bi, i: (bi * nq + i, 0)),
                  pl.BlockSpec((t, LANES), lambda bi, i: (bi, 0))],
        out_specs=pl.BlockSpec((None, nc, tq, kc), lambda bi, i: (bi * nq + i, 0, 0, 0)),
        scratch_shapes=[pltpu.VMEM((nc, tq, kc), jnp.int32)],
        compiler_params=_params("parallel", "arbitrary"),
    )(qi, kw, kw)


def _prompt_attn_kernel(*refs, mode, t, kc):
    if mode == "swa":
        sink_ref, q_ref, k_ref, v_ref, o_ref = refs
    elif mode == "dsa":
        q_ref, k_ref, v_ref, mask_ref, o_ref = refs
    else:
        q_ref, k_ref, v_ref, o_ref, kmean_ref = refs
    i = pl.program_id(1)
    tq = q_ref.shape[0]
    rows = GROUP * tq
    row = lax.broadcasted_iota(jnp.int32, (rows, 1), 0)
    qpos = i * tq + _mod_pow2(row, tq)
    kcol = lax.broadcasted_iota(jnp.int32, (rows, kc), 1)

    if mode == "moba":
        n_blk = t // MOBA_BLOCK

        @pl.when(i == 0)
        def _():
            kmean_ref[...] = jnp.zeros_like(kmean_ref)
            for g in range(N_KV_HEADS):
                kg = k_ref[:, g * HEAD_DIM:(g + 1) * HEAD_DIM]
                ksum = jnp.sum(kg.reshape(n_blk, MOBA_BLOCK, HEAD_DIM), axis=1)
                kmean_ref[g, 0:n_blk, :] = ksum * (1.0 / MOBA_BLOCK)

    if mode == "swa":
        lo, hi = jnp.maximum(i - 1, 0), i + 1
    else:
        lo, hi = 0, _div_pow2((i + 1) * tq + kc - 1, kc)

    for g in range(N_KV_HEADS):
        gsl = slice(g * HEAD_DIM, (g + 1) * HEAD_DIM)
        qg = jnp.concatenate(
            [q_ref[:, (g * GROUP + n) * HEAD_DIM:(g * GROUP + n + 1) * HEAD_DIM] for n in range(GROUP)], axis=0)
        qb = (qg * ATTN_SCALE).astype(BF16)
        if mode == "moba":
            gate = lax.dot_general(qg, kmean_ref[g], _NT, precision=lax.Precision.HIGHEST,
                                   preferred_element_type=F32)
            allow = _top3_blocks(gate, _div_pow2(qpos, MOBA_BLOCK), n_blk)
            blk_lane = lax.broadcasted_iota(jnp.int32, allow.shape, 1)

        def body(c, carry, g=g, gsl=gsl, qb=qb):
            m, l, acc = carry
            k0 = pl.multiple_of(c * kc, kc)
            kb = k_ref[pl.ds(k0, kc), gsl].astype(BF16)
            vb = v_ref[pl.ds(k0, kc), gsl].astype(BF16)
            s = lax.dot_general(qb, kb, _NT, preferred_element_type=F32)
            kpos = kcol + k0
            if mode == "dsa":
                mk = mask_ref[c]
                mask = jnp.concatenate([mk] * GROUP, axis=0).astype(F32) > 0.5
            elif mode == "moba":
                picked = jnp.max(jnp.where(blk_lane == c, allow, 0.0), axis=-1, keepdims=True)
                mask = (picked > 0.5) & (kpos <= qpos)
            else:
                diff = qpos - kpos
                mask = (diff >= 0) & (diff < WINDOW)
            return _softmax_step(s, mask, vb, m, l, acc)

        if mode == "swa":
            m0 = _per_head_rows(sink_ref, g, _div_pow2(row, tq))
            l0 = jnp.ones((rows, 1), F32)
        else:
            m0 = jnp.full((rows, 1), NEG_INF, F32)
            l0 = jnp.zeros((rows, 1), F32)
        m, l, acc = lax.fori_loop(lo, hi, body, (m0, l0, jnp.zeros((rows, HEAD_DIM), F32)))
        o = acc / l
        for n in range(GROUP):
            hsl = slice((g * GROUP + n) * HEAD_DIM, (g * GROUP + n + 1) * HEAD_DIM)
            o_ref[:, hsl] = o[n * tq:(n + 1) * tq]


def _prompt_attn(qkv, b, t, mode, mask=None, sinks=None):
    tq = ATTN_Q_TILE
    nq = t // tq
    kc = {"dsa": DSA_KEY_CHUNK, "moba": MOBA_BLOCK, "swa": WINDOW}[mode]
    in_specs = [pl.BlockSpec((tq, Q_DIM), lambda bi, i: (bi * nq + i, 0)),
                pl.BlockSpec((t, KV_DIM), lambda bi, i: (bi, Q_DIM // KV_DIM)),
                pl.BlockSpec((t, KV_DIM), lambda bi, i: (bi, Q_DIM // KV_DIM + 1))]
    args = [qkv, qkv, qkv]
    scratch = []
    if mode == "dsa":
        nc = t // kc
        in_specs.append(pl.BlockSpec((None, nc, tq, kc), lambda bi, i: (bi * nq + i, 0, 0, 0)))
        args.append(mask)
    elif mode == "moba":
        assert t % MOBA_BLOCK == 0 and t // MOBA_BLOCK <= LANES
        scratch.append(pltpu.VMEM((N_KV_HEADS, LANES, HEAD_DIM), F32))
    else:
        assert tq == WINDOW
        in_specs.insert(0, pl.BlockSpec(memory_space=pltpu.SMEM))
        args.insert(0, sinks)
    return pl.pallas_call(
        functools.partial(_prompt_attn_kernel, mode=mode, t=t, kc=kc),
        out_shape=jax.ShapeDtypeStruct((b * t, Q_DIM), F32),
        grid=(b, nq),
        in_specs=in_specs,
        out_specs=pl.BlockSpec((tq, Q_DIM), lambda bi, i: (bi * nq + i, 0)),
        scratch_shapes=scratch,
        compiler_params=_params("parallel", "arbitrary"),
    )(*args)


def _page_specs(shape_tail, n_pages, pages_per_step):
    def spec(j):
        def index(bi, c, pt):
            page = jnp.minimum(c * pages_per_step + j, n_pages - 1)
            return (pt[bi * n_pages + page],) + (0,) * len(shape_tail)
        return pl.BlockSpec((None,) + shape_tail, index)
    return [spec(j) for j in range(pages_per_step)]


def _dsa_sample_select_kernel(pt_ref, qi_ref, w_ref, *refs, pages, topk, s_len):
    ik_refs = refs[:pages]
    knew_ref, mask_ref, key_ref = refs[pages:]
    c = pl.program_id(1)
    last = pl.num_programs(1) - 1
    nsteps, srows, width = key_ref.shape
    rows = mask_ref.shape[1]
    qi = qi_ref[...]
    w = w_ref[...]

    def scores(ki_t):
        d = jnp.dot(qi, ki_t, precision=lax.Precision.HIGHEST, preferred_element_type=F32)
        r = jnp.maximum(d, 0.0) * w
        srow = lax.broadcasted_iota(jnp.int32, (srows, ki_t.shape[1]), 0)
        sc = jnp.zeros((srows, ki_t.shape[1]), F32)
        for s in range(s_len):
            tot = jnp.sum(r[s * IDX_HEADS:(s + 1) * IDX_HEADS], axis=0, keepdims=True)
            sc = jnp.where(srow == s, tot, sc)
        return sc

    srow_w = lax.broadcasted_iota(jnp.int32, (srows, width), 0)
    new_ok = (lax.broadcasted_iota(jnp.int32, (srows, width), 1) <= srow_w) & (srow_w < s_len)

    @pl.when(c < last)
    def _():
        key_ref[c] = _sortable_key(scores(jnp.concatenate([r[...] for r in ik_refs], axis=1)))

    @pl.when(c == last)
    def _():
        sc = jnp.concatenate([scores(knew_ref[...]), jnp.zeros((srows, width - PAGE_SIZE), F32)], axis=1)
        key_ref[c] = _sortable_key(jnp.where(new_ok, sc, NEG_INF))
        thr = _kth_largest_key(key_ref, topk, srows)
        tok = _mod_pow2(lax.broadcasted_iota(jnp.int32, (rows, width), 0), s_len)
        for cc in range(nsteps):
            keep = key_ref[cc] >= thr
            if cc == nsteps - 1:
                keep = keep & new_ok
            keep = jnp.where(keep, 1.0, 0.0)
            out = jnp.zeros((rows, width), F32)
            for s in range(s_len):
                out = jnp.where(tok == s, keep[s:s + 1, :], out)
            mask_ref[0, :, cc * width:(cc + 1) * width] = out.astype(BF16)


def _dsa_select_sample(pt_flat, qi_s, w_s, cache_ik_t, ki_new_t, n_pages, s_len):
    db = qi_s.shape[0]
    rows = GROUP * s_len
    srows = 8
    pages = min(PAGES_PER_STEP, n_pages)
    assert n_pages % pages == 0 and s_len <= srows and pages > 1
    nsteps = n_pages // pages + 1
    width = pages * PAGE_SIZE
    lp = nsteps * width
    bsel = lambda bi, c, pt: (bi, 0, 0)
    grid_spec = pltpu.PrefetchScalarGridSpec(
        num_scalar_prefetch=1,
        grid=(db, nsteps),
        in_specs=[pl.BlockSpec((None, s_len * IDX_HEADS, IDX_DIM), bsel),
                  pl.BlockSpec((None, s_len * IDX_HEADS, 1), bsel)]
        + _page_specs((IDX_DIM, PAGE_SIZE), n_pages, pages)
        + [pl.BlockSpec((None, IDX_DIM, PAGE_SIZE), bsel)],
        out_specs=pl.BlockSpec((None, 1, rows, lp), lambda bi, c, pt: (bi, 0, 0, 0)),
        scratch_shapes=[pltpu.VMEM((nsteps, srows, width), jnp.int32)],
    )
    topk = min(DSA_TOPK_MAX, (n_pages * PAGE_SIZE + s_len) // 4)
    return pl.pallas_call(
        functools.partial(_dsa_sample_select_kernel, pages=pages, topk=topk, s_len=s_len),
        out_shape=jax.ShapeDtypeStruct((db, 1, rows, lp), BF16),
        grid_spec=grid_spec,
        compiler_params=_params("parallel", "arbitrary"),
    )(pt_flat, qi_s, w_s, *([cache_ik_t] * pages), ki_new_t)


def _moba_block_mean_kernel(pt_ref, *refs, pages):
    k_refs = refs[:pages]
    o_ref = refs[pages]
    per_blk = MOBA_BLOCK // PAGE_SIZE
    for j in range(pages // per_blk):
        tot = None
        for u in range(per_blk):
            page = k_refs[j * per_blk + u][...]
            part = jnp.sum(page.reshape(page.shape[0] // 8, 8, HEAD_DIM), axis=0)
            tot = part if tot is None else tot + part
        o_ref[j * 8:(j + 1) * 8, :] = tot * (1.0 / MOBA_BLOCK)


def _moba_block_means(pt_flat, cache_k, db, n_pages):
    pages = MOBA_SUM_PAGES
    assert n_pages % pages == 0 and 2 * N_KV_HEADS == 8
    per_step = pages * PAGE_SIZE // MOBA_BLOCK
    grid_spec = pltpu.PrefetchScalarGridSpec(
        num_scalar_prefetch=1,
        grid=(db, n_pages // pages),
        in_specs=_page_specs((PAGE_SIZE * N_KV_HEADS, HEAD_DIM), n_pages, pages),
        out_specs=pl.BlockSpec((None, per_step * 8, HEAD_DIM), lambda bi, c, pt: (bi, c, 0)),
    )
    return pl.pallas_call(
        functools.partial(_moba_block_mean_kernel, pages=pages),
        out_shape=jax.ShapeDtypeStruct((db, n_pages * PAGE_SIZE // MOBA_BLOCK * 8, HEAD_DIM), F32),
        grid_spec=grid_spec,
        compiler_params=_params("parallel", "arbitrary"),
    )(pt_flat, *([cache_k] * pages))


def _moba_sample_gate_kernel(q_ref, kmean_ref, allow_ref, *, n_past_blk, n_blk, s_len, past_len):
    rows = q_ref.shape[1]
    tok = _mod_pow2(lax.broadcasted_iota(jnp.int32, (rows, 1), 0), s_len)
    own = _div_pow2(past_len + tok, MOBA_BLOCK)
    for g in range(N_KV_HEADS):
        km = (kmean_ref[pl.ds(g, n_past_blk, stride=8), :]
              + kmean_ref[pl.ds(g + N_KV_HEADS, n_past_blk, stride=8), :])
        km = jnp.concatenate([km, jnp.zeros((LANES - n_past_blk, HEAD_DIM), F32)], axis=0)
        gate = lax.dot_general(q_ref[g], km, _NT, precision=lax.Precision.HIGHEST, preferred_element_type=F32)
        allow_ref[g] = _top3_blocks(gate, own, n_blk)


def _moba_sample_gate(qs, kmean, n_pages, s_len):
    db, _, rows, _ = qs.shape
    past_len = n_pages * PAGE_SIZE
    n_past_blk = kmean.shape[1] // 8
    n_blk = -(-(past_len + s_len) // MOBA_BLOCK)
    assert past_len % MOBA_BLOCK == 0 and s_len <= MOBA_BLOCK and n_blk <= LANES
    return pl.pallas_call(
        functools.partial(_moba_sample_gate_kernel, n_past_blk=n_past_blk, n_blk=n_blk, s_len=s_len,
                          past_len=past_len),
        out_shape=jax.ShapeDtypeStruct((db, N_KV_HEADS, rows, LANES), F32),
        grid=(db,),
        in_specs=[pl.BlockSpec((None, N_KV_HEADS, rows, HEAD_DIM), lambda bi: (bi, 0, 0, 0)),
                  pl.BlockSpec((None, n_past_blk * 8, HEAD_DIM), lambda bi: (bi, 0, 0))],
        out_specs=pl.BlockSpec((None, N_KV_HEADS, rows, LANES), lambda bi: (bi, 0, 0, 0)),
        compiler_params=_params("parallel"),
    )(qs, kmean)


def _paged_attn_kernel(pt_ref, *refs, pages, mode, use_sink, s_len, past_len):
    if use_sink:
        sink_ref, refs = refs[0], refs[1:]
    q_ref = refs[0]
    k_refs = refs[1:1 + pages]
    v_refs = refs[1 + pages:1 + 2 * pages]
    knew_ref, vnew_ref, sel_ref, o_ref, m_ref, l_ref, acc_ref = refs[1 + 2 * pages:]
    c = pl.program_id(1)
    last = pl.num_programs(1) - 1
    rows = q_ref.shape[1]
    all_rows = N_KV_HEADS * rows
    row = lax.broadcasted_iota(jnp.int32, (all_rows, 1), 0)
    tok = _mod_pow2(row, s_len)

    @pl.when(c == 0)
    def _():
        if use_sink:
            head = _div_pow2(row, s_len)
            m0 = jnp.full((all_rows, 1), sink_ref[N_HEADS - 1], F32)
            for h in range(N_HEADS - 1):
                m0 = jnp.where(head == h, sink_ref[h], m0)
            m_ref[...] = m0
            l_ref[...] = jnp.ones((all_rows, 1), F32)
        else:
            m_ref[...] = jnp.full((all_rows, 1), NEG_INF, F32)
            l_ref[...] = jnp.zeros((all_rows, 1), F32)
        acc_ref[...] = jnp.zeros((all_rows, HEAD_DIM), F32)

    def block_allowed(blk):
        lane = lax.broadcasted_iota(jnp.int32, (all_rows, LANES), 1)
        allow = sel_ref[...].reshape(all_rows, LANES)
        return jnp.max(jnp.where(lane == blk, allow, 0.0), axis=-1, keepdims=True) > 0.5

    def head_rows(page_refs, g):
        return jnp.concatenate([r[pl.ds(g, PAGE_SIZE, stride=N_KV_HEADS), :] for r in page_refs],
                               axis=0).astype(BF16)

    def update(k_pages, v_pages, mask):
        qb = (q_ref[...].reshape(all_rows, HEAD_DIM) * ATTN_SCALE).astype(BF16)
        s = jnp.concatenate(
            [lax.dot_general(qb[g * rows:(g + 1) * rows], head_rows(k_pages, g), _NT, preferred_element_type=F32)
             for g in range(N_KV_HEADS)], axis=0)
        p, m, l, alpha = _softmax_probs(s, mask, m_ref[...], l_ref[...])
        pb = p.astype(BF16)
        pv = jnp.concatenate(
            [jnp.dot(pb[g * rows:(g + 1) * rows], head_rows(v_pages, g), preferred_element_type=F32)
             for g in range(N_KV_HEADS)], axis=0)
        m_ref[...] = m
        l_ref[...] = l
        acc_ref[...] = alpha * acc_ref[...] + pv

    @pl.when(c < last)
    def _():
        if mode == "mask":
            mask = jnp.concatenate([sel_ref[0]] * N_KV_HEADS, axis=0).astype(F32) > 0.5
        else:
            mask = jnp.concatenate(
                [jnp.broadcast_to(block_allowed(_div_pow2(c * pages + j, MOBA_BLOCK // PAGE_SIZE)),
                                  (all_rows, PAGE_SIZE)) for j in range(pages)], axis=1)
        update(k_refs, v_refs, mask)

    @pl.when(c == last)
    def _():
        if mode == "mask":
            mask = jnp.concatenate([sel_ref[0][:, :PAGE_SIZE]] * N_KV_HEADS, axis=0).astype(F32) > 0.5
        else:
            col = lax.broadcasted_iota(jnp.int32, (all_rows, PAGE_SIZE), 1)
            mask = block_allowed(past_len // MOBA_BLOCK) & (col <= tok)
        update([knew_ref], [vnew_ref], mask)
        o_ref[...] = (acc_ref[...] / l_ref[...]).reshape(N_KV_HEADS, rows, HEAD_DIM)


def _paged_attn(pt_flat, qs, cache_k, cache_v, k_new, v_new, sel, n_pages, s_len, mode, sinks=None):
    db, _, rows, _ = qs.shape
    pages = min(PAGES_PER_STEP, n_pages)
    assert n_pages % pages == 0
    nsteps = n_pages // pages + 1
    width = pages * PAGE_SIZE
    bsel3 = lambda bi, c, pt: (bi, 0, 0)
    bsel4 = lambda bi, c, pt: (bi, 0, 0, 0)
    if mode == "mask":
        sel_spec = pl.BlockSpec((None, 1, rows, width), lambda bi, c, pt: (bi, 0, 0, c))
    else:
        assert (n_pages * PAGE_SIZE) % MOBA_BLOCK == 0 and s_len <= MOBA_BLOCK
        sel_spec = pl.BlockSpec((None, N_KV_HEADS, rows, LANES), bsel4)
    page_shape = (PAGE_SIZE * N_KV_HEADS, HEAD_DIM)
    in_specs = ([pl.BlockSpec((None, N_KV_HEADS, rows, HEAD_DIM), bsel4)]
                + _page_specs(page_shape, n_pages, pages) + _page_specs(page_shape, n_pages, pages)
                + [pl.BlockSpec((None,) + page_shape, bsel3)] * 2 + [sel_spec])
    args = [qs] + [cache_k] * pages + [cache_v] * pages + [k_new, v_new, sel]
    if sinks is not None:
        in_specs.insert(0, pl.BlockSpec(memory_space=pltpu.SMEM))
        args.insert(0, sinks)
    grid_spec = pltpu.PrefetchScalarGridSpec(
        num_scalar_prefetch=1,
        grid=(db, nsteps),
        in_specs=in_specs,
        out_specs=pl.BlockSpec((None, N_KV_HEADS, rows, HEAD_DIM), bsel4),
        scratch_shapes=[pltpu.VMEM((N_KV_HEADS * rows, 1), F32), pltpu.VMEM((N_KV_HEADS * rows, 1), F32),
                        pltpu.VMEM((N_KV_HEADS * rows, HEAD_DIM), F32)],
    )
    return pl.pallas_call(
        functools.partial(_paged_attn_kernel, pages=pages, mode=mode, use_sink=sinks is not None,
                          s_len=s_len, past_len=n_pages * PAGE_SIZE),
        out_shape=jax.ShapeDtypeStruct((db, N_KV_HEADS, rows, HEAD_DIM), F32),
        grid_spec=grid_spec,
        compiler_params=_params("parallel", "arbitrary"),
    )(pt_flat, *args)


def _pad_rows(a, rows):
    return jnp.pad(a, ((0, 0), (0, rows - a.shape[1]), (0, 0)))


def _as_pages(cache):
    assert cache.shape[1:] == (PAGE_SIZE, N_KV_HEADS, HEAD_DIM), cache.shape
    return cache.reshape(cache.shape[0], PAGE_SIZE * N_KV_HEADS, HEAD_DIM)


def kernel(x_prompt, x_sample, cache_k_0, cache_v_0, cache_ik_0, cache_k_1, cache_v_1, state_wk_2, state_wv_2,
           cache_k_3, cache_v_3, cache_ik_3, page_table, ln_attn, w_qkv, q_norm, k_norm, w_o, w_idx, idx_ln_g,
           idx_ln_b, sinks, ln_ffn, w_gate_up, w_down):
    b, t, d = x_prompt.shape
    db, s_len, _ = x_sample.shape
    n_pages = page_table.shape[1]
    past_len = n_pages * PAGE_SIZE
    depth = w_qkv.shape[0]
    n_p = b * t
    n = n_p + db * s_len
    tm = _row_tile(n)
    rows = GROUP * s_len
    assert t % DSA_KEY_CHUNK == 0 and t % ATTN_Q_TILE == 0 and rows % 8 == 0

    caches = {0: (cache_k_0, cache_v_0, cache_ik_0), 1: (cache_k_1, cache_v_1), 2: (state_wk_2, state_wv_2),
              3: (cache_k_3, cache_v_3, cache_ik_3)}
    pt_flat = page_table.reshape(-1).astype(jnp.int32)
    win_pt = jnp.arange(db, dtype=jnp.int32)

    pos = jnp.concatenate([jnp.tile(jnp.arange(t, dtype=jnp.int32), b),
                           jnp.tile(past_len + jnp.arange(s_len, dtype=jnp.int32), db)])
    tabs_main = _rope_tables(pos, ROPE_DIM // 2, HEAD_DIM)
    tabs_iq = _rope_tables(pos, IDX_ROPE_DIM // 2, IDX_DIM)
    tabs_ik = _rope_tables(pos, IDX_ROPE_DIM // 2, LANES)

    tok = (jnp.arange(rows, dtype=jnp.int32) % s_len)[:, None]
    col = jnp.arange(PAGE_SIZE, dtype=jnp.int32)[None, :]
    swa_mask = jnp.concatenate([col > tok + (WINDOW - PAGE_SIZE), col <= tok], axis=1)
    swa_mask = jnp.broadcast_to(swa_mask.astype(BF16)[None, None], (db, 1, rows, 2 * PAGE_SIZE))

    x = jnp.concatenate([x_prompt.reshape(n_p, d), x_sample.reshape(db * s_len, d)], axis=0)
    outs = {}
    for i in range(depth):
        kind, j = i % 3, i // 3
        gain = jnp.concatenate([jnp.tile(q_norm[i], N_HEADS), jnp.tile(k_norm[i], N_KV_HEADS),
                                jnp.ones((KV_DIM,), F32)])[None, :]
        ln_a = ln_attn[i][None, :]
        qkv = _qkv_proj(x, ln_a, w_qkv[i].astype(BF16), gain, tabs_main, tm)
        k_all = qkv[:, Q_DIM:Q_DIM + KV_DIM]
        v_all = qkv[:, Q_DIM + KV_DIM:]
        k_p = k_all[:n_p].reshape(b, t, N_KV_HEADS, HEAD_DIM)
        v_p = v_all[:n_p].reshape(b, t, N_KV_HEADS, HEAD_DIM)
        k_s = k_all[n_p:].reshape(db, s_len, N_KV_HEADS, HEAD_DIM)
        v_s = v_all[n_p:].reshape(db, s_len, N_KV_HEADS, HEAD_DIM)
        k_new = _pad_rows(k_all[n_p:].reshape(db, s_len * N_KV_HEADS, HEAD_DIM), PAGE_SIZE * N_KV_HEADS)
        v_new = _pad_rows(v_all[n_p:].reshape(db, s_len * N_KV_HEADS, HEAD_DIM), PAGE_SIZE * N_KV_HEADS)
        qs = qkv[n_p:, :Q_DIM].reshape(db, s_len, N_KV_HEADS, GROUP, HEAD_DIM)
        qs = qs.transpose(0, 2, 3, 1, 4).reshape(db, N_KV_HEADS, rows, HEAD_DIM)

        if kind == 0:
            ck, cv, cik = caches[i]
            w_i = jnp.pad(w_idx[j], ((0, 0), (0, IDX_PROJ_PAD - IDX_PROJ))).astype(BF16)
            lg = jnp.pad(idx_ln_g[j], (0, LANES - IDX_DIM))[None, :]
            lb = jnp.pad(idx_ln_b[j], (0, LANES - IDX_DIM))[None, :]
            qi, kw = _idx_proj(x, ln_a, w_i, lg, lb, tabs_iq, tabs_ik, tm)
            mask_p = _dsa_select_prompt(qi, kw, b, t)
            o_p = _prompt_attn(qkv, b, t, "dsa", mask=mask_p)
            qi_s = qi[n_p:].reshape(db, s_len * IDX_HEADS, IDX_DIM)
            w_s = kw[n_p:, IDX_DIM:IDX_DIM + IDX_HEADS].reshape(db, s_len * IDX_HEADS, 1)
            ki_new_t = _pad_rows(kw[n_p:, :IDX_DIM].reshape(db, s_len, IDX_DIM), PAGE_SIZE).transpose(0, 2, 1)
            mask_s = _dsa_select_sample(pt_flat, qi_s, w_s, cik.transpose(0, 2, 1), ki_new_t, n_pages, s_len)
            o_s = _paged_attn(pt_flat, qs, _as_pages(ck), _as_pages(cv), k_new, v_new, mask_s, n_pages, s_len,
                              "mask")
            ik_p = kw[:n_p, :IDX_DIM].reshape(b, t, IDX_DIM)
            ik_s = kw[n_p:, :IDX_DIM].reshape(db, s_len, IDX_DIM)
            outs[i] = (k_p, v_p, ik_p, k_s, v_s, ik_s)
        elif kind == 1:
            ck, cv = caches[i]
            ck2 = _as_pages(ck)
            cv2 = _as_pages(cv)
            o_p = _prompt_attn(qkv, b, t, "moba")
            kmean = _moba_block_means(pt_flat, ck2, db, n_pages)
            allow = _moba_sample_gate(qs, kmean, n_pages, s_len)
            o_s = _paged_attn(pt_flat, qs, ck2, cv2, k_new, v_new, allow, n_pages, s_len, "block")
            outs[i] = (k_p, v_p, k_s, v_s)
        else:
            bk, bv = caches[i]
            o_p = _prompt_attn(qkv, b, t, "swa", sinks=sinks[j])
            o_s = _paged_attn(win_pt, qs, _as_pages(bk), _as_pages(bv), k_new, v_new, swa_mask, 1, s_len, "mask",
                              sinks=sinks[j])
            outs[i] = (k_p[:, -WINDOW:], v_p[:, -WINDOW:],
                       jnp.concatenate([bk, k_s], axis=1)[:, -WINDOW:],
                       jnp.concatenate([bv, v_s], axis=1)[:, -WINDOW:])

        o_s = o_s.reshape(db, N_KV_HEADS, GROUP, s_len, HEAD_DIM).transpose(0, 3, 1, 2, 4).reshape(db * s_len, Q_DIM)
        o = jnp.concatenate([o_p, o_s], axis=0)
        x = _out_proj(o, w_o[i].astype(BF16), x, tm)
        x = _ffn(x, ln_ffn[i][None, :], w_gate_up[i].astype(BF16), w_down[i].astype(BF16), tm)

    y_p = x[:n_p].reshape(b, t, d)
    y_s = x[n_p:].reshape(db, s_len, d)
    return (y_p, y_s) + outs[0] + outs[1] + outs[2] + outs[3]
```

```python
import functools

import jax
import jax.numpy as jnp
from jax import lax
from jax.experimental import pallas as pl
from jax.experimental.pallas import tpu as pltpu

F32 = jnp.float32
BF16 = jnp.bfloat16

N_HEADS = 16
N_KV_HEADS = 4
HEAD_DIM = 128
GROUP = N_HEADS // N_KV_HEADS
Q_DIM = N_HEADS * HEAD_DIM
KV_DIM = N_KV_HEADS * HEAD_DIM
QKV_DIM = Q_DIM + 2 * KV_DIM
ROPE_THETA = 500000.0
ROPE_DIM = HEAD_DIM // 4
IDX_HEADS = 16
IDX_DIM = 64
IDX_ROPE_DIM = IDX_DIM // 4
IDX_Q = IDX_HEADS * IDX_DIM
IDX_PROJ = IDX_Q + IDX_DIM + IDX_HEADS
IDX_PROJ_PAD = IDX_Q + 128
DSA_TOPK_MAX = 256
MOBA_BLOCK = 256
MOBA_TOPK = 3
WINDOW = 128
PAGE_SIZE = 128
RMS_EPS = 1e-6
NEG_INF = -1e30
ATTN_SCALE = HEAD_DIM ** -0.5
INT32_MIN = -(2 ** 31)

LANES = 128
VMEM_LIMIT = 56 * 1024 * 1024
ROW_TILE_MAX = 640
LOG2E = 1.4426950408889634
QKV_COL_TILE = 512
FFN_COL_TILE = 512
ATTN_Q_TILE = 128
DSA_KEY_CHUNK = 512
PAGES_PER_STEP = 16
MOBA_SUM_PAGES = 16

_NT = (((1,), (1,)), ((), ()))


def _params(*sem):
    return pltpu.CompilerParams(dimension_semantics=sem, vmem_limit_bytes=VMEM_LIMIT)


def _log2(v):
    assert v > 0 and v & (v - 1) == 0, v
    return v.bit_length() - 1


def _div_pow2(x, d):
    return x >> _log2(d)


def _mod_pow2(x, d):
    _log2(d)
    return x & (d - 1)


def _per_head_rows(sink_ref, g, row_head):
    out = jnp.full(row_head.shape, sink_ref[g * GROUP + GROUP - 1], F32)
    for n in range(GROUP - 1):
        out = jnp.where(row_head == n, sink_ref[g * GROUP + n], out)
    return out


def _row_tile(n):
    best = None
    for t in range(LANES, ROW_TILE_MAX + 1, LANES):
        if n % t == 0:
            best = t
    assert best is not None, n
    return best


def _rope_tables(pos, half, period):
    inv_freq = ROPE_THETA ** (-jnp.arange(half, dtype=F32) / half)
    ang = pos.astype(F32)[:, None] * inv_freq[None, :]
    cos = jnp.cos(ang)
    sin = jnp.sin(ang)
    r = jnp.arange(LANES) % period
    lo = r < half
    hi = (r >= half) & (r < 2 * half)
    idx = jnp.clip(jnp.where(lo, r, r - half), 0, half - 1)
    cg = cos[:, idx]
    sg = sin[:, idx]
    return (jnp.where(lo | hi, cg, 1.0).astype(F32),
            jnp.where(lo, -sg, 0.0).astype(F32),
            jnp.where(hi, sg, 0.0).astype(F32))


def _rope(y, cos, sin_a, sin_b, half):
    return y * cos + pltpu.roll(y, LANES - half, 1) * sin_a + pltpu.roll(y, half, 1) * sin_b


def _rms_rows_bf16(x_ref, ln_ref):
    xf = x_ref[...]
    ms = jnp.mean(xf * xf, axis=-1, keepdims=True)
    return (xf * lax.rsqrt(ms + RMS_EPS) * ln_ref[...]).astype(BF16)


def _qkv_kernel(x_ref, ln_ref, w_ref, g_ref, cos_ref, sa_ref, sb_ref, o_ref, h_ref, *, n_norm_cols):
    j = pl.program_id(1)

    @pl.when(j == 0)
    def _():
        h_ref[...] = _rms_rows_bf16(x_ref, ln_ref)

    tn = o_ref.shape[1]
    acc = jnp.dot(h_ref[...], w_ref[...], preferred_element_type=F32)

    @pl.when((j + 1) * tn <= n_norm_cols)
    def _():
        cos = cos_ref[...]
        sa = sa_ref[...]
        sb = sb_ref[...]
        for hh in range(tn // HEAD_DIM):
            sl = slice(hh * HEAD_DIM, (hh + 1) * HEAD_DIM)
            a = acc[:, sl]
            ms = jnp.mean(a * a, axis=-1, keepdims=True)
            y = a * lax.rsqrt(ms + RMS_EPS) * g_ref[:, sl]
            o_ref[:, sl] = _rope(y, cos, sa, sb, ROPE_DIM // 2)

    @pl.when(j * tn >= n_norm_cols)
    def _():
        o_ref[...] = acc


def _qkv_proj(x, ln, w, gain, tabs, tm):
    n, d = x.shape
    tn = QKV_COL_TILE
    tab_spec = pl.BlockSpec((tm, LANES), lambda i, j: (i, 0))
    return pl.pallas_call(
        functools.partial(_qkv_kernel, n_norm_cols=Q_DIM + KV_DIM),
        out_shape=jax.ShapeDtypeStruct((n, QKV_DIM), F32),
        grid=(n // tm, QKV_DIM // tn),
        in_specs=[pl.BlockSpec((tm, d), lambda i, j: (i, 0)),
                  pl.BlockSpec((1, d), lambda i, j: (0, 0)),
                  pl.BlockSpec((d, tn), lambda i, j: (0, j)),
                  pl.BlockSpec((1, tn), lambda i, j: (0, j)),
                  tab_spec, tab_spec, tab_spec],
        out_specs=pl.BlockSpec((tm, tn), lambda i, j: (i, j)),
        scratch_shapes=[pltpu.VMEM((tm, d), BF16)],
        compiler_params=_params("parallel", "arbitrary"),
    )(x, ln, w, gain, *tabs)


def _idx_kernel(x_ref, ln_ref, w_ref, lg_ref, lb_ref, cq_ref, saq_ref, sbq_ref,
                ck_ref, sak_ref, sbk_ref, qi_ref, kw_ref):
    h = _rms_rows_bf16(x_ref, ln_ref)
    acc = jnp.dot(h, w_ref[...], preferred_element_type=F32)
    half = IDX_ROPE_DIM // 2
    cq = cq_ref[...]
    saq = saq_ref[...]
    sbq = sbq_ref[...]
    for c in range(IDX_Q // LANES):
        sl = slice(c * LANES, (c + 1) * LANES)
        qi_ref[:, sl] = _rope(acc[:, sl], cq, saq, sbq, half)
    a = acc[:, IDX_Q:IDX_Q + LANES]
    lane = lax.broadcasted_iota(jnp.int32, a.shape, 1)
    is_key = lane < IDX_DIM
    mu = jnp.sum(jnp.where(is_key, a, 0.0), axis=-1, keepdims=True) * (1.0 / IDX_DIM)
    xc = a - mu
    var = jnp.sum(jnp.where(is_key, xc * xc, 0.0), axis=-1, keepdims=True) * (1.0 / IDX_DIM)
    y = xc * lax.rsqrt(var + RMS_EPS) * lg_ref[...] + lb_ref[...]
    kn = jnp.where(is_key, y, a * (IDX_HEADS ** -0.5 * IDX_DIM ** -0.5))
    kw_ref[...] = _rope(kn, ck_ref[...], sak_ref[...], sbk_ref[...], half)


def _idx_proj(x, ln, w, lg, lb, tabs_q, tabs_k, tm):
    n, d = x.shape
    row = lambda i: (i, 0)
    fix = lambda i: (0, 0)
    tab_spec = pl.BlockSpec((tm, LANES), row)
    return pl.pallas_call(
        _idx_kernel,
        out_shape=(jax.ShapeDtypeStruct((n, IDX_Q), F32), jax.ShapeDtypeStruct((n, LANES), F32)),
        grid=(n // tm,),
        in_specs=[pl.BlockSpec((tm, d), row), pl.BlockSpec((1, d), fix),
                  pl.BlockSpec((d, IDX_PROJ_PAD), fix),
                  pl.BlockSpec((1, LANES), fix), pl.BlockSpec((1, LANES), fix)] + [tab_spec] * 6,
        out_specs=(pl.BlockSpec((tm, IDX_Q), row), pl.BlockSpec((tm, LANES), row)),
        compiler_params=_params("parallel"),
    )(x, ln, w, lg, lb, *tabs_q, *tabs_k)


def _oproj_kernel(o_ref, w_ref, x_ref, out_ref, ob_ref):
    @pl.when(pl.program_id(1) == 0)
    def _():
        ob_ref[...] = o_ref[...].astype(BF16)

    out_ref[...] = x_ref[...] + jnp.dot(ob_ref[...], w_ref[...], preferred_element_type=F32)


def _out_proj(o, w, x, tm):
    n, d = x.shape
    tn = min(QKV_COL_TILE, d)
    return pl.pallas_call(
        _oproj_kernel,
        out_shape=jax.ShapeDtypeStruct((n, d), F32),
        grid=(n // tm, d // tn),
        in_specs=[pl.BlockSpec((tm, Q_DIM), lambda i, j: (i, 0)),
                  pl.BlockSpec((Q_DIM, tn), lambda i, j: (0, j)),
                  pl.BlockSpec((tm, tn), lambda i, j: (i, j))],
        out_specs=pl.BlockSpec((tm, tn), lambda i, j: (i, j)),
        scratch_shapes=[pltpu.VMEM((tm, Q_DIM), BF16)],
        compiler_params=_params("parallel", "arbitrary"),
    )(o, w, x)


def _ffn_kernel(x_ref, ln_ref, wg_ref, wu_ref, wd_ref, o_ref, h_ref, acc_ref):
    f = pl.program_id(1)

    @pl.when(f == 0)
    def _():
        h_ref[...] = _rms_rows_bf16(x_ref, ln_ref)
        acc_ref[...] = jnp.zeros_like(acc_ref)

    h = h_ref[...]
    g = jnp.dot(h, wg_ref[...], preferred_element_type=F32)
    u = jnp.dot(h, wu_ref[...], preferred_element_type=F32)
    a = (g * jax.nn.sigmoid(g)) * u
    acc_ref[...] += jnp.dot(a.astype(BF16), wd_ref[...], preferred_element_type=F32)

    @pl.when(f == pl.num_programs(1) - 1)
    def _():
        o_ref[...] = x_ref[...] + acc_ref[...]


def _ffn(x, ln, w_gu, w_d, tm):
    n, d = x.shape
    d_ff = w_d.shape[0]
    tf = min(FFN_COL_TILE, d_ff)
    nf = d_ff // tf
    return pl.pallas_call(
        _ffn_kernel,
        out_shape=jax.ShapeDtypeStruct((n, d), F32),
        grid=(n // tm, nf),
        in_specs=[pl.BlockSpec((tm, d), lambda i, f: (i, 0)),
                  pl.BlockSpec((1, d), lambda i, f: (0, 0)),
                  pl.BlockSpec((d, tf), lambda i, f: (0, f)),
                  pl.BlockSpec((d, tf), lambda i, f: (0, f + nf)),
                  pl.BlockSpec((tf, d), lambda i, f: (f, 0))],
        out_specs=pl.BlockSpec((tm, d), lambda i, f: (i, 0)),
        scratch_shapes=[pltpu.VMEM((tm, d), BF16), pltpu.VMEM((tm, d), F32)],
        compiler_params=_params("parallel", "arbitrary"),
    )(x, ln, w_gu, w_gu, w_d)


def _sortable_key(score):
    bits = pltpu.bitcast(score, jnp.int32)
    return jnp.where(bits < 0, bits ^ jnp.int32(0x7FFFFFFF), bits)


def _kth_largest_key(key_ref, k, rows, n_chunks=None):
    width = key_ref.shape[2]

    def count_ge(cand):
        if n_chunks is None:
            ge = jnp.where(key_ref[...] >= cand[None], 1.0, 0.0)
            return jnp.sum(jnp.sum(ge, axis=0), axis=-1, keepdims=True)

        def add_chunk(c, part):
            ge = jnp.where(key_ref[c] >= cand, 1.0, 0.0)
            for w in range(width // LANES):
                part = part + ge[:, w * LANES:(w + 1) * LANES]
            return part

        part = lax.fori_loop(0, n_chunks, add_chunk, jnp.zeros((rows, LANES), F32))
        return jnp.sum(part, axis=-1, keepdims=True)

    zero = jnp.zeros((rows, 1), jnp.int32)
    t0 = jnp.where(count_ge(zero) >= k, zero, jnp.int32(INT32_MIN))

    def body(it, t):
        cand = t | lax.shift_left(jnp.int32(1), jnp.int32(30) - it)
        return jnp.where(count_ge(cand) >= k, cand, t)

    return lax.fori_loop(0, 31, body, t0)


def _split_bf16(x):
    hi = x.astype(BF16)
    return hi, (x - hi.astype(F32)).astype(BF16)


def _index_scores(qi, wq, ki):
    sc = None
    for h in range(IDX_HEADS):
        qh = qi[:, h * IDX_DIM:(h + 1) * IDX_DIM].astype(BF16)
        d = lax.dot_general(qh, ki, _NT, preferred_element_type=F32)
        t = jnp.maximum(d, 0.0) * wq[:, IDX_DIM + h:IDX_DIM + h + 1]
        sc = t if sc is None else sc + t
    return sc


def _top3_blocks(gate, own, n_blk):
    lane = lax.broadcasted_iota(jnp.int32, gate.shape, 1)
    lane_f = lane.astype(F32)
    gm = jnp.where(lane < own, gate, NEG_INF)
    gm = jnp.where(lane < n_blk, gm, -jnp.inf)
    sel = jnp.zeros(gate.shape, F32)
    for _ in range(MOBA_TOPK):
        mx = jnp.max(gm, axis=-1, keepdims=True)
        first = jnp.min(jnp.where(gm == mx, lane_f, float(LANES)), axis=-1, keepdims=True)
        pick = lane_f == first
        sel = jnp.where(pick, 1.0, sel)
        gm = jnp.where(pick, -jnp.inf, gm)
    return jnp.where(lane < own, sel, jnp.where(lane == own, 1.0, 0.0))


def _softmax_probs(s, mask, m, l):
    for mk in (mask if isinstance(mask, tuple) else (mask,)):
        s = jnp.where(mk, s, NEG_INF)
    mn = jnp.maximum(m, jnp.max(s, axis=-1, keepdims=True))
    alpha = jnp.exp2(m - mn)
    p = jnp.exp2(s - mn)
    return p, mn, alpha * l + jnp.sum(p, axis=-1, keepdims=True), alpha


def _softmax_step(s, mask, vb, m, l, acc):
    p, mn, l, alpha = _softmax_probs(s, mask, m, l)
    acc = alpha * acc + jnp.dot(p.astype(BF16), vb, preferred_element_type=F32)
    return mn, l, acc


def _dsa_select_kernel(qi_ref, wq_ref, kk_ref, mask_ref, key_ref, *, topk):
    i = pl.program_id(1)
    nc, tq, kc = key_ref.shape
    qi = qi_ref[...]
    wq = wq_ref[...]
    qpos = i * tq + lax.broadcasted_iota(jnp.int32, (tq, 1), 0)
    col = lax.broadcasted_iota(jnp.int32, (tq, kc), 1)

    n_live = _div_pow2((i + 1) * tq + kc - 1, kc)

    def chunk(c, carry):
        k0 = pl.multiple_of(c * kc, kc)
        ki = kk_ref[pl.ds(k0, kc), :][:, :IDX_DIM].astype(BF16)
        sc = jnp.where(col + k0 <= qpos, _index_scores(qi, wq, ki), NEG_INF)
        key_ref[c] = _sortable_key(sc)
        return carry

    lax.fori_loop(0, n_live, chunk, 0)
    thr = _kth_largest_key(key_ref, topk, tq, n_chunks=n_live)

    def emit(c, carry):
        keep = jnp.where(key_ref[c] >= thr, 1.0, 0.0)
        mask_ref[c] = jnp.where(col + c * kc <= qpos, keep, 0.0).astype(BF16)
        return carry

    lax.fori_loop(0, n_live, emit, 0)

    def blank(c, carry):
        mask_ref[c] = jnp.zeros((tq, kc), BF16)
        return carry

    lax.fori_loop(n_live, nc, blank, 0)


def _dsa_select_prompt(qi, kw, b, t):
    tq, kc = ATTN_Q_TILE, DSA_KEY_CHUNK
    nq, nc = t // tq, t // kc
    topk = min(DSA_TOPK_MAX, t // 4)
    assert kc >= topk
    return pl.pallas_call(
        functools.partial(_dsa_select_kernel, topk=topk),
        out_shape=jax.ShapeDtypeStruct((b * nq, nc, tq, kc), BF16),
        grid=(b, nq),
        in_specs=[pl.BlockSpec((tq, IDX_Q), lambda bi, i: (bi * nq + i, 0)),
                  pl.BlockSpec((tq, LANES), lambda bi, i: (bi * nq + i, 0)),
                  pl.BlockSpec((t, LANES), lambda bi, i: (bi, 0))],
        out_specs=pl.BlockSpec((None, nc, tq, kc), lambda bi, i: (bi * nq + i, 0, 0, 0)),
        scratch_shapes=[pltpu.VMEM((nc, tq, kc), jnp.int32)],
        compiler_params=_params("parallel", "arbitrary"),
    )(qi, kw, kw)


def _prompt_attn_kernel(*refs, mode, t, kc):
    if mode == "swa":
        sink_ref, q_ref, k_ref, v_ref, o_ref = refs
    elif mode == "dsa":
        q_ref, k_ref, v_ref, mask_ref, o_ref = refs
    else:
        q_ref, k_ref, v_ref, o_ref, kmean_ref = refs
    i = pl.program_id(1)
    tq = q_ref.shape[0]
    rows = GROUP * tq
    row = lax.broadcasted_iota(jnp.int32, (rows, 1), 0)
    qpos = i * tq + _mod_pow2(row, tq)
    kcol = lax.broadcasted_iota(jnp.int32, (rows, kc), 1)

    if mode == "moba":
        n_blk = t // MOBA_BLOCK

        @pl.when(i == 0)
        def _():
            kmean_ref[...] = jnp.zeros_like(kmean_ref)
            for g in range(N_KV_HEADS):
                kg = k_ref[:, g * HEAD_DIM:(g + 1) * HEAD_DIM]
                ksum = jnp.sum(kg.reshape(n_blk, MOBA_BLOCK, HEAD_DIM), axis=1)
                kmean_ref[g, 0:n_blk, :] = ksum * (1.0 / MOBA_BLOCK)

    assert kc % tq == 0
    last = _div_pow2(i * tq, kc)
    first = jnp.maximum(last - 1, 0) if mode == "swa" else 0

    for g in range(N_KV_HEADS):
        gsl = slice(g * HEAD_DIM, (g + 1) * HEAD_DIM)
        qg = jnp.concatenate(
            [q_ref[:, (g * GROUP + n) * HEAD_DIM:(g * GROUP + n + 1) * HEAD_DIM] for n in range(GROUP)], axis=0)
        qb = (qg * (ATTN_SCALE * LOG2E)).astype(BF16)
        if mode == "moba":
            gate = lax.dot_general(qg, kmean_ref[g], _NT, precision=lax.Precision.HIGHEST,
                                   preferred_element_type=F32)
            allow = _top3_blocks(gate, _div_pow2(qpos, MOBA_BLOCK), n_blk)
            blk_lane = lax.broadcasted_iota(jnp.int32, allow.shape, 1)

        def step(c, carry, diagonal, g=g, gsl=gsl, qb=qb):
            m, l, acc = carry
            k0 = pl.multiple_of(c * kc, kc)
            kb = k_ref[pl.ds(k0, kc), gsl].astype(BF16)
            vb = v_ref[pl.ds(k0, kc), gsl].astype(BF16)
            s = lax.dot_general(qb, kb, _NT, preferred_element_type=F32)
            if mode == "dsa":
                mask = jnp.concatenate([mask_ref[c]] * GROUP, axis=0).astype(F32) > 0.5
            elif mode == "moba":
                picked = jnp.max(jnp.where(blk_lane == c, allow, 0.0), axis=-1, keepdims=True) > 0.5
                mask = (picked, kcol + k0 <= qpos) if diagonal else picked
            elif diagonal:
                mask = kcol + k0 <= qpos
            else:
                mask = qpos - (kcol + k0) < WINDOW
            return _softmax_step(s, mask, vb, m, l, acc)

        if mode == "swa":
            m0 = _per_head_rows(sink_ref, g, _div_pow2(row, tq)) * LOG2E
            l0 = jnp.ones((rows, 1), F32)
        else:
            m0 = jnp.full((rows, 1), NEG_INF, F32)
            l0 = jnp.zeros((rows, 1), F32)
        carry = lax.fori_loop(first, last, functools.partial(step, diagonal=False),
                              (m0, l0, jnp.zeros((rows, HEAD_DIM), F32)))
        m, l, acc = step(last, carry, True)
        o = acc / l
        for n in range(GROUP):
            hsl = slice((g * GROUP + n) * HEAD_DIM, (g * GROUP + n + 1) * HEAD_DIM)
            o_ref[:, hsl] = o[n * tq:(n + 1) * tq]


def _prompt_attn(qkv, b, t, mode, mask=None, sinks=None):
    tq = ATTN_Q_TILE
    nq = t // tq
    kc = {"dsa": DSA_KEY_CHUNK, "moba": MOBA_BLOCK, "swa": WINDOW}[mode]
    in_specs = [pl.BlockSpec((tq, Q_DIM), lambda bi, i: (bi * nq + i, 0)),
                pl.BlockSpec((t, KV_DIM), lambda bi, i: (bi, Q_DIM // KV_DIM)),
                pl.BlockSpec((t, KV_DIM), lambda bi, i: (bi, Q_DIM // KV_DIM + 1))]
    args = [qkv, qkv, qkv]
    scratch = []
    if mode == "dsa":
        nc = t // kc
        in_specs.append(pl.BlockSpec((None, nc, tq, kc), lambda bi, i: (bi * nq + i, 0, 0, 0)))
        args.append(mask)
    elif mode == "moba":
        assert t % MOBA_BLOCK == 0 and t // MOBA_BLOCK <= LANES
        scratch.append(pltpu.VMEM((N_KV_HEADS, LANES, HEAD_DIM), F32))
    else:
        assert tq == WINDOW
        in_specs.insert(0, pl.BlockSpec(memory_space=pltpu.SMEM))
        args.insert(0, sinks)
    return pl.pallas_call(
        functools.partial(_prompt_attn_kernel, mode=mode, t=t, kc=kc),
        out_shape=jax.ShapeDtypeStruct((b * t, Q_DIM), F32),
        grid=(b, nq),
        in_specs=in_specs,
        out_specs=pl.BlockSpec((tq, Q_DIM), lambda bi, i: (bi * nq + i, 0)),
        scratch_shapes=scratch,
        compiler_params=_params("parallel", "arbitrary"),
    )(*args)


def _page_specs(shape_tail, n_pages, pages_per_step):
    def spec(j):
        def index(bi, c, pt):
            page = jnp.minimum(c * pages_per_step + j, n_pages - 1)
            return (pt[bi * n_pages + page],) + (0,) * len(shape_tail)
        return pl.BlockSpec((None,) + shape_tail, index)
    return [spec(j) for j in range(pages_per_step)]


def _dsa_sample_select_kernel(pt_ref, qi_ref, w_ref, *refs, pages, topk, s_len):
    ik_refs = refs[:pages]
    knew_ref, mask_ref, key_ref = refs[pages:]
    c = pl.program_id(1)
    last = pl.num_programs(1) - 1
    nsteps, srows, width = key_ref.shape
    rows = mask_ref.shape[1]
    qi = qi_ref[...]
    w = w_ref[...]

    q_hi, q_lo = _split_bf16(qi)

    def scores(ki_t):
        k_hi, k_lo = _split_bf16(ki_t)
        d = (jnp.dot(q_hi, k_hi, preferred_element_type=F32) + jnp.dot(q_hi, k_lo, preferred_element_type=F32)
             + jnp.dot(q_lo, k_hi, preferred_element_type=F32))
        r = jnp.maximum(d, 0.0) * w
        srow = lax.broadcasted_iota(jnp.int32, (srows, ki_t.shape[1]), 0)
        sc = jnp.zeros((srows, ki_t.shape[1]), F32)
        for s in range(s_len):
            tot = jnp.sum(r[s * IDX_HEADS:(s + 1) * IDX_HEADS], axis=0, keepdims=True)
            sc = jnp.where(srow == s, tot, sc)
        return sc

    srow_w = lax.broadcasted_iota(jnp.int32, (srows, width), 0)
    new_ok = (lax.broadcasted_iota(jnp.int32, (srows, width), 1) <= srow_w) & (srow_w < s_len)

    @pl.when(c < last)
    def _():
        key_ref[c] = _sortable_key(scores(jnp.concatenate([r[...] for r in ik_refs], axis=1)))

    @pl.when(c == last)
    def _():
        sc = jnp.concatenate([scores(knew_ref[...]), jnp.zeros((srows, width - PAGE_SIZE), F32)], axis=1)
        key_ref[c] = _sortable_key(jnp.where(new_ok, sc, NEG_INF))
        thr = _kth_largest_key(key_ref, topk, srows)
        tok = _mod_pow2(lax.broadcasted_iota(jnp.int32, (rows, width), 0), s_len)
        for cc in range(nsteps):
            keep = key_ref[cc] >= thr
            if cc == nsteps - 1:
                keep = keep & new_ok
            keep = jnp.where(keep, 1.0, 0.0)
            out = jnp.zeros((rows, width), F32)
            for s in range(s_len):
                out = jnp.where(tok == s, keep[s:s + 1, :], out)
            mask_ref[0, :, cc * width:(cc + 1) * width] = out.astype(BF16)


def _dsa_select_sample(pt_flat, qi_s, w_s, cache_ik_t, ki_new_t, n_pages, s_len):
    db = qi_s.shape[0]
    rows = GROUP * s_len
    srows = 8
    pages = min(PAGES_PER_STEP, n_pages)
    assert n_pages % pages == 0 and s_len <= srows and pages > 1
    nsteps = n_pages // pages + 1
    width = pages * PAGE_SIZE
    lp = nsteps * width
    bsel = lambda bi, c, pt: (bi, 0, 0)
    grid_spec = pltpu.PrefetchScalarGridSpec(
        num_scalar_prefetch=1,
        grid=(db, nsteps),
        in_specs=[pl.BlockSpec((None, s_len * IDX_HEADS, IDX_DIM), bsel),
                  pl.BlockSpec((None, s_len * IDX_HEADS, 1), bsel)]
        + _page_specs((IDX_DIM, PAGE_SIZE), n_pages, pages)
        + [pl.BlockSpec((None, IDX_DIM, PAGE_SIZE), bsel)],
        out_specs=pl.BlockSpec((None, 1, rows, lp), lambda bi, c, pt: (bi, 0, 0, 0)),
        scratch_shapes=[pltpu.VMEM((nsteps, srows, width), jnp.int32)],
    )
    topk = min(DSA_TOPK_MAX, (n_pages * PAGE_SIZE + s_len) // 4)
    return pl.pallas_call(
        functools.partial(_dsa_sample_select_kernel, pages=pages, topk=topk, s_len=s_len),
        out_shape=jax.ShapeDtypeStruct((db, 1, rows, lp), BF16),
        grid_spec=grid_spec,
        compiler_params=_params("parallel", "arbitrary"),
    )(pt_flat, qi_s, w_s, *([cache_ik_t] * pages), ki_new_t)


def _moba_block_mean_kernel(pt_ref, *refs, pages):
    k_refs = refs[:pages]
    o_ref = refs[pages]
    per_blk = MOBA_BLOCK // PAGE_SIZE
    for j in range(pages // per_blk):
        tot = None
        for u in range(per_blk):
            page = k_refs[j * per_blk + u][...]
            part = jnp.sum(page.reshape(page.shape[0] // 8, 8, HEAD_DIM), axis=0)
            tot = part if tot is None else tot + part
        o_ref[j * 8:(j + 1) * 8, :] = tot * (1.0 / MOBA_BLOCK)


def _moba_block_means(pt_flat, cache_k, db, n_pages):
    pages = MOBA_SUM_PAGES
    assert n_pages % pages == 0 and 2 * N_KV_HEADS == 8
    per_step = pages * PAGE_SIZE // MOBA_BLOCK
    grid_spec = pltpu.PrefetchScalarGridSpec(
        num_scalar_prefetch=1,
        grid=(db, n_pages // pages),
        in_specs=_page_specs((PAGE_SIZE * N_KV_HEADS, HEAD_DIM), n_pages, pages),
        out_specs=pl.BlockSpec((None, per_step * 8, HEAD_DIM), lambda bi, c, pt: (bi, c, 0)),
    )
    return pl.pallas_call(
        functools.partial(_moba_block_mean_kernel, pages=pages),
        out_shape=jax.ShapeDtypeStruct((db, n_pages * PAGE_SIZE // MOBA_BLOCK * 8, HEAD_DIM), F32),
        grid_spec=grid_spec,
        compiler_params=_params("parallel", "arbitrary"),
    )(pt_flat, *([cache_k] * pages))


def _moba_sample_gate_kernel(q_ref, kmean_ref, allow_ref, *, n_past_blk, n_blk, s_len, past_len):
    rows = q_ref.shape[1]
    tok = _mod_pow2(lax.broadcasted_iota(jnp.int32, (rows, 1), 0), s_len)
    own = _div_pow2(past_len + tok, MOBA_BLOCK)
    for g in range(N_KV_HEADS):
        km = (kmean_ref[pl.ds(g, n_past_blk, stride=8), :]
              + kmean_ref[pl.ds(g + N_KV_HEADS, n_past_blk, stride=8), :])
        km = jnp.concatenate([km, jnp.zeros((LANES - n_past_blk, HEAD_DIM), F32)], axis=0)
        gate = lax.dot_general(q_ref[g], km, _NT, precision=lax.Precision.HIGHEST, preferred_element_type=F32)
        allow_ref[g] = _top3_blocks(gate, own, n_blk)


def _moba_sample_gate(qs, kmean, n_pages, s_len):
    db, _, rows, _ = qs.shape
    past_len = n_pages * PAGE_SIZE
    n_past_blk = kmean.shape[1] // 8
    n_blk = -(-(past_len + s_len) // MOBA_BLOCK)
    assert past_len % MOBA_BLOCK == 0 and s_len <= MOBA_BLOCK and n_blk <= LANES
    return pl.pallas_call(
        functools.partial(_moba_sample_gate_kernel, n_past_blk=n_past_blk, n_blk=n_blk, s_len=s_len,
                          past_len=past_len),
        out_shape=jax.ShapeDtypeStruct((db, N_KV_HEADS, rows, LANES), F32),
        grid=(db,),
        in_specs=[pl.BlockSpec((None, N_KV_HEADS, rows, HEAD_DIM), lambda bi: (bi, 0, 0, 0)),
                  pl.BlockSpec((None, n_past_blk * 8, HEAD_DIM), lambda bi: (bi, 0, 0))],
        out_specs=pl.BlockSpec((None, N_KV_HEADS, rows, LANES), lambda bi: (bi, 0, 0, 0)),
        compiler_params=_params("parallel"),
    )(qs, kmean)


def _paged_attn_kernel(pt_ref, *refs, pages, mode, use_sink, s_len, past_len):
    if use_sink:
        sink_ref, refs = refs[0], refs[1:]
    q_ref = refs[0]
    k_refs = refs[1:1 + pages]
    v_refs = refs[1 + pages:1 + 2 * pages]
    knew_ref, vnew_ref, sel_ref, o_ref, m_ref, l_ref, acc_ref = refs[1 + 2 * pages:]
    c = pl.program_id(1)
    last = pl.num_programs(1) - 1
    rows = q_ref.shape[1]
    all_rows = N_KV_HEADS * rows
    row = lax.broadcasted_iota(jnp.int32, (all_rows, 1), 0)
    tok = _mod_pow2(row, s_len)

    @pl.when(c == 0)
    def _():
        if use_sink:
            head = _div_pow2(row, s_len)
            m0 = jnp.full((all_rows, 1), sink_ref[N_HEADS - 1], F32)
            for h in range(N_HEADS - 1):
                m0 = jnp.where(head == h, sink_ref[h], m0)
            m_ref[...] = m0 * LOG2E
            l_ref[...] = jnp.ones((all_rows, 1), F32)
        else:
            m_ref[...] = jnp.full((all_rows, 1), NEG_INF, F32)
            l_ref[...] = jnp.zeros((all_rows, 1), F32)
        acc_ref[...] = jnp.zeros((all_rows, HEAD_DIM), F32)

    def block_allowed(blk):
        lane = lax.broadcasted_iota(jnp.int32, (all_rows, LANES), 1)
        allow = sel_ref[...].reshape(all_rows, LANES)
        return jnp.max(jnp.where(lane == blk, allow, 0.0), axis=-1, keepdims=True) > 0.5

    def head_rows(page_refs, g):
        return jnp.concatenate([r[pl.ds(g, PAGE_SIZE, stride=N_KV_HEADS), :] for r in page_refs],
                               axis=0).astype(BF16)

    def update(k_pages, v_pages, mask):
        qb = (q_ref[...].reshape(all_rows, HEAD_DIM) * (ATTN_SCALE * LOG2E)).astype(BF16)
        s = jnp.concatenate(
            [lax.dot_general(qb[g * rows:(g + 1) * rows], head_rows(k_pages, g), _NT, preferred_element_type=F32)
             for g in range(N_KV_HEADS)], axis=0)
        p, m, l, alpha = _softmax_probs(s, mask, m_ref[...], l_ref[...])
        pb = p.astype(BF16)
        pv = jnp.concatenate(
            [jnp.dot(pb[g * rows:(g + 1) * rows], head_rows(v_pages, g), preferred_element_type=F32)
             for g in range(N_KV_HEADS)], axis=0)
        m_ref[...] = m
        l_ref[...] = l
        acc_ref[...] = alpha * acc_ref[...] + pv

    @pl.when(c < last)
    def _():
        if mode == "mask":
            mask = jnp.concatenate([sel_ref[0]] * N_KV_HEADS, axis=0).astype(F32) > 0.5
        else:
            mask = jnp.concatenate(
                [jnp.broadcast_to(block_allowed(_div_pow2(c * pages + j, MOBA_BLOCK // PAGE_SIZE)),
                                  (all_rows, PAGE_SIZE)) for j in range(pages)], axis=1)
        update(k_refs, v_refs, mask)

    @pl.when(c == last)
    def _():
        if mode == "mask":
            mask = jnp.concatenate([sel_ref[0][:, :PAGE_SIZE]] * N_KV_HEADS, axis=0).astype(F32) > 0.5
        else:
            col = lax.broadcasted_iota(jnp.int32, (all_rows, PAGE_SIZE), 1)
            mask = (block_allowed(past_len // MOBA_BLOCK), col <= tok)
        update([knew_ref], [vnew_ref], mask)
        o_ref[...] = (acc_ref[...] / l_ref[...]).reshape(N_KV_HEADS, rows, HEAD_DIM)


def _paged_attn(pt_flat, qs, cache_k, cache_v, k_new, v_new, sel, n_pages, s_len, mode, sinks=None):
    db, _, rows, _ = qs.shape
    pages = min(PAGES_PER_STEP, n_pages)
    assert n_pages % pages == 0
    nsteps = n_pages // pages + 1
    width = pages * PAGE_SIZE
    bsel3 = lambda bi, c, pt: (bi, 0, 0)
    bsel4 = lambda bi, c, pt: (bi, 0, 0, 0)
    if mode == "mask":
        sel_spec = pl.BlockSpec((None, 1, rows, width), lambda bi, c, pt: (bi, 0, 0, c))
    else:
        assert (n_pages * PAGE_SIZE) % MOBA_BLOCK == 0 and s_len <= MOBA_BLOCK
        sel_spec = pl.BlockSpec((None, N_KV_HEADS, rows, LANES), bsel4)
    page_shape = (PAGE_SIZE * N_KV_HEADS, HEAD_DIM)
    in_specs = ([pl.BlockSpec((None, N_KV_HEADS, rows, HEAD_DIM), bsel4)]
                + _page_specs(page_shape, n_pages, pages) + _page_specs(page_shape, n_pages, pages)
                + [pl.BlockSpec((None,) + page_shape, bsel3)] * 2 + [sel_spec])
    args = [qs] + [cache_k] * pages + [cache_v] * pages + [k_new, v_new, sel]
    if sinks is not None:
        in_specs.insert(0, pl.BlockSpec(memory_space=pltpu.SMEM))
        args.insert(0, sinks)
    grid_spec = pltpu.PrefetchScalarGridSpec(
        num_scalar_prefetch=1,
        grid=(db, nsteps),
        in_specs=in_specs,
        out_specs=pl.BlockSpec((None, N_KV_HEADS, rows, HEAD_DIM), bsel4),
        scratch_shapes=[pltpu.VMEM((N_KV_HEADS * rows, 1), F32), pltpu.VMEM((N_KV_HEADS * rows, 1), F32),
                        pltpu.VMEM((N_KV_HEADS * rows, HEAD_DIM), F32)],
    )
    return pl.pallas_call(
        functools.partial(_paged_attn_kernel, pages=pages, mode=mode, use_sink=sinks is not None,
                          s_len=s_len, past_len=n_pages * PAGE_SIZE),
        out_shape=jax.ShapeDtypeStruct((db, N_KV_HEADS, rows, HEAD_DIM), F32),
        grid_spec=grid_spec,
        compiler_params=_params("parallel", "arbitrary"),
    )(pt_flat, *args)


def _pad_rows(a, rows):
    return jnp.pad(a, ((0, 0), (0, rows - a.shape[1]), (0, 0)))


def _as_pages(cache):
    assert cache.shape[1:] == (PAGE_SIZE, N_KV_HEADS, HEAD_DIM), cache.shape
    return cache.reshape(cache.shape[0], PAGE_SIZE * N_KV_HEADS, HEAD_DIM)


def kernel(x_prompt, x_sample, cache_k_0, cache_v_0, cache_ik_0, cache_k_1, cache_v_1, state_wk_2, state_wv_2,
           cache_k_3, cache_v_3, cache_ik_3, page_table, ln_attn, w_qkv, q_norm, k_norm, w_o, w_idx, idx_ln_g,
           idx_ln_b, sinks, ln_ffn, w_gate_up, w_down):
    b, t, d = x_prompt.shape
    db, s_len, _ = x_sample.shape
    n_pages = page_table.shape[1]
    past_len = n_pages * PAGE_SIZE
    depth = w_qkv.shape[0]
    n_p = b * t
    n = n_p + db * s_len
    tm = _row_tile(n)
    rows = GROUP * s_len
    assert t % DSA_KEY_CHUNK == 0 and t % ATTN_Q_TILE == 0 and rows % 8 == 0

    caches = {0: (cache_k_0, cache_v_0, cache_ik_0), 1: (cache_k_1, cache_v_1), 2: (state_wk_2, state_wv_2),
              3: (cache_k_3, cache_v_3, cache_ik_3)}
    pt_flat = page_table.reshape(-1).astype(jnp.int32)
    win_pt = jnp.arange(db, dtype=jnp.int32)

    pos = jnp.concatenate([jnp.tile(jnp.arange(t, dtype=jnp.int32), b),
                           jnp.tile(past_len + jnp.arange(s_len, dtype=jnp.int32), db)])
    tabs_main = _rope_tables(pos, ROPE_DIM // 2, HEAD_DIM)
    tabs_iq = _rope_tables(pos, IDX_ROPE_DIM // 2, IDX_DIM)
    tabs_ik = _rope_tables(pos, IDX_ROPE_DIM // 2, LANES)

    tok = (jnp.arange(rows, dtype=jnp.int32) % s_len)[:, None]
    col = jnp.arange(PAGE_SIZE, dtype=jnp.int32)[None, :]
    swa_mask = jnp.concatenate([col > tok + (WINDOW - PAGE_SIZE), col <= tok], axis=1)
    swa_mask = jnp.broadcast_to(swa_mask.astype(BF16)[None, None], (db, 1, rows, 2 * PAGE_SIZE))

    x = jnp.concatenate([x_prompt.reshape(n_p, d), x_sample.reshape(db * s_len, d)], axis=0)
    outs = {}
    for i in range(depth):
        kind, j = i % 3, i // 3
        gain = jnp.concatenate([jnp.tile(q_norm[i], N_HEADS), jnp.tile(k_norm[i], N_KV_HEADS),
                                jnp.ones((KV_DIM,), F32)])[None, :]
        ln_a = ln_attn[i][None, :]
        qkv = _qkv_proj(x, ln_a, w_qkv[i].astype(BF16), gain, tabs_main, tm)
        k_all = qkv[:, Q_DIM:Q_DIM + KV_DIM]
        v_all = qkv[:, Q_DIM + KV_DIM:]
        k_p = k_all[:n_p].reshape(b, t, N_KV_HEADS, HEAD_DIM)
        v_p = v_all[:n_p].reshape(b, t, N_KV_HEADS, HEAD_DIM)
        k_s = k_all[n_p:].reshape(db, s_len, N_KV_HEADS, HEAD_DIM)
        v_s = v_all[n_p:].reshape(db, s_len, N_KV_HEADS, HEAD_DIM)
        k_new = _pad_rows(k_all[n_p:].reshape(db, s_len * N_KV_HEADS, HEAD_DIM), PAGE_SIZE * N_KV_HEADS)
        v_new = _pad_rows(v_all[n_p:].reshape(db, s_len * N_KV_HEADS, HEAD_DIM), PAGE_SIZE * N_KV_HEADS)
        qs = qkv[n_p:, :Q_DIM].reshape(db, s_len, N_KV_HEADS, GROUP, HEAD_DIM)
        qs = qs.transpose(0, 2, 3, 1, 4).reshape(db, N_KV_HEADS, rows, HEAD_DIM)

        if kind == 0:
            ck, cv, cik = caches[i]
            w_i = jnp.pad(w_idx[j], ((0, 0), (0, IDX_PROJ_PAD - IDX_PROJ))).astype(BF16)
            lg = jnp.pad(idx_ln_g[j], (0, LANES - IDX_DIM))[None, :]
            lb = jnp.pad(idx_ln_b[j], (0, LANES - IDX_DIM))[None, :]
            qi, kw = _idx_proj(x, ln_a, w_i, lg, lb, tabs_iq, tabs_ik, tm)
            mask_p = _dsa_select_prompt(qi, kw, b, t)
            o_p = _prompt_attn(qkv, b, t, "dsa", mask=mask_p)
            qi_s = qi[n_p:].reshape(db, s_len * IDX_HEADS, IDX_DIM)
            w_s = kw[n_p:, IDX_DIM:IDX_DIM + IDX_HEADS].reshape(db, s_len * IDX_HEADS, 1)
            ki_new_t = _pad_rows(kw[n_p:, :IDX_DIM].reshape(db, s_len, IDX_DIM), PAGE_SIZE).transpose(0, 2, 1)
            mask_s = _dsa_select_sample(pt_flat, qi_s, w_s, cik.transpose(0, 2, 1), ki_new_t, n_pages, s_len)
            o_s = _paged_attn(pt_flat, qs, _as_pages(ck), _as_pages(cv), k_new, v_new, mask_s, n_pages, s_len,
                              "mask")
            ik_p = kw[:n_p, :IDX_DIM].reshape(b, t, IDX_DIM)
            ik_s = kw[n_p:, :IDX_DIM].reshape(db, s_len, IDX_DIM)
            outs[i] = (k_p, v_p, ik_p, k_s, v_s, ik_s)
        elif kind == 1:
            ck, cv = caches[i]
            ck2 = _as_pages(ck)
            cv2 = _as_pages(cv)
            o_p = _prompt_attn(qkv, b, t, "moba")
            kmean = _moba_block_means(pt_flat, ck2, db, n_pages)
            allow = _moba_sample_gate(qs, kmean, n_pages, s_len)
            o_s = _paged_attn(pt_flat, qs, ck2, cv2, k_new, v_new, allow, n_pages, s_len, "block")
            outs[i] = (k_p, v_p, k_s, v_s)
        else:
            bk, bv = caches[i]
            o_p = _prompt_attn(qkv, b, t, "swa", sinks=sinks[j])
            o_s = _paged_attn(win_pt, qs, _as_pages(bk), _as_pages(bv), k_new, v_new, swa_mask, 1, s_len, "mask",
                              sinks=sinks[j])
            outs[i] = (k_p[:, -WINDOW:], v_p[:, -WINDOW:],
                       jnp.concatenate([bk, k_s], axis=1)[:, -WINDOW:],
                       jnp.concatenate([bv, v_s], axis=1)[:, -WINDOW:])

        o_s = o_s.reshape(db, N_KV_HEADS, GROUP, s_len, HEAD_DIM).transpose(0, 3, 1, 2, 4).reshape(db * s_len, Q_DIM)
        o = jnp.concatenate([o_p, o_s], axis=0)
        x = _out_proj(o, w_o[i].astype(BF16), x, tm)
        x = _ffn(x, ln_ffn[i][None, :], w_gate_up[i].astype(BF16), w_down[i].astype(BF16), tm)

    y_p = x[:n_p].reshape(b, t, d)
    y_s = x[n_p:].reshape(db, s_len, d)
    return (y_p, y_s) + outs[0] + outs[1] + outs[2] + outs[3]
```

```python
import functools

import jax
import jax.numpy as jnp
from jax import lax
from jax.experimental import pallas as pl
from jax.experimental.pallas import tpu as pltpu

F32 = jnp.float32
BF16 = jnp.bfloat16

N_HEADS = 16
N_KV_HEADS = 4
HEAD_DIM = 128
GROUP = N_HEADS // N_KV_HEADS
Q_DIM = N_HEADS * HEAD_DIM
KV_DIM = N_KV_HEADS * HEAD_DIM
QKV_DIM = Q_DIM + 2 * KV_DIM
ROPE_THETA = 500000.0
ROPE_DIM = HEAD_DIM // 4
IDX_HEADS = 16
IDX_DIM = 64
IDX_ROPE_DIM = IDX_DIM // 4
IDX_Q = IDX_HEADS * IDX_DIM
IDX_PROJ = IDX_Q + IDX_DIM + IDX_HEADS
IDX_PROJ_PAD = IDX_Q + 128
DSA_TOPK_MAX = 256
MOBA_BLOCK = 256
MOBA_TOPK = 3
WINDOW = 128
PAGE_SIZE = 128
RMS_EPS = 1e-6
NEG_INF = -1e30
ATTN_SCALE = HEAD_DIM ** -0.5
INT32_MIN = -(2 ** 31)

LANES = 128
VMEM_LIMIT = 56 * 1024 * 1024
ROW_TILE_MAX = 640
LOG2E = 1.4426950408889634
QKV_COL_TILE = 512
FFN_COL_TILE = 512
ATTN_Q_TILE = 128
DSA_KEY_CHUNK = 512
PAGES_PER_STEP = 16
MOBA_SUM_PAGES = 16

_NT = (((1,), (1,)), ((), ()))
_TN = (((0,), (0,)), ((), ()))


def _params(*sem):
    return pltpu.CompilerParams(dimension_semantics=sem, vmem_limit_bytes=VMEM_LIMIT)


def _log2(v):
    assert v > 0 and v & (v - 1) == 0, v
    return v.bit_length() - 1


def _div_pow2(x, d):
    return x >> _log2(d)


def _mod_pow2(x, d):
    _log2(d)
    return x & (d - 1)


def _row_tile(n):
    best = None
    for t in range(LANES, ROW_TILE_MAX + 1, LANES):
        if n % t == 0:
            best = t
    assert best is not None, n
    return best


def _rope_tables(pos, half, period):
    inv_freq = ROPE_THETA ** (-jnp.arange(half, dtype=F32) / half)
    ang = pos.astype(F32)[:, None] * inv_freq[None, :]
    cos = jnp.cos(ang)
    sin = jnp.sin(ang)
    r = jnp.arange(LANES) % period
    lo = r < half
    hi = (r >= half) & (r < 2 * half)
    idx = jnp.clip(jnp.where(lo, r, r - half), 0, half - 1)
    cg = cos[:, idx]
    sg = sin[:, idx]
    return (jnp.where(lo | hi, cg, 1.0).astype(F32),
            jnp.where(lo, -sg, 0.0).astype(F32),
            jnp.where(hi, sg, 0.0).astype(F32))


def _rope(y, cos, sin_a, sin_b, half):
    return y * cos + pltpu.roll(y, LANES - half, 1) * sin_a + pltpu.roll(y, half, 1) * sin_b


def _rms_rows_bf16(x_ref, ln_ref):
    xf = x_ref[...]
    ms = jnp.mean(xf * xf, axis=-1, keepdims=True)
    return (xf * lax.rsqrt(ms + RMS_EPS) * ln_ref[...]).astype(BF16)


def _qkv_kernel(x_ref, ln_ref, w_ref, g_ref, cos_ref, sa_ref, sb_ref, o_ref, h_ref, *, n_norm_cols):
    j = pl.program_id(1)

    @pl.when(j == 0)
    def _():
        h_ref[...] = _rms_rows_bf16(x_ref, ln_ref)

    tn = o_ref.shape[1]
    acc = jnp.dot(h_ref[...], w_ref[...], preferred_element_type=F32)

    @pl.when((j + 1) * tn <= n_norm_cols)
    def _():
        cos = cos_ref[...]
        sa = sa_ref[...]
        sb = sb_ref[...]
        for hh in range(tn // HEAD_DIM):
            sl = slice(hh * HEAD_DIM, (hh + 1) * HEAD_DIM)
            a = acc[:, sl]
            ms = jnp.mean(a * a, axis=-1, keepdims=True)
            y = a * lax.rsqrt(ms + RMS_EPS) * g_ref[:, sl]
            o_ref[:, sl] = _rope(y, cos, sa, sb, ROPE_DIM // 2)

    @pl.when(j * tn >= n_norm_cols)
    def _():
        o_ref[...] = acc


def _qkv_proj(x, ln, w, gain, tabs, tm):
    n, d = x.shape
    tn = QKV_COL_TILE
    tab_spec = pl.BlockSpec((tm, LANES), lambda i, j: (i, 0))
    return pl.pallas_call(
        functools.partial(_qkv_kernel, n_norm_cols=Q_DIM + KV_DIM),
        out_shape=jax.ShapeDtypeStruct((n, QKV_DIM), F32),
        grid=(n // tm, QKV_DIM // tn),
        in_specs=[pl.BlockSpec((tm, d), lambda i, j: (i, 0)),
                  pl.BlockSpec((1, d), lambda i, j: (0, 0)),
                  pl.BlockSpec((d, tn), lambda i, j: (0, j)),
                  pl.BlockSpec((1, tn), lambda i, j: (0, j)),
                  tab_spec, tab_spec, tab_spec],
        out_specs=pl.BlockSpec((tm, tn), lambda i, j: (i, j)),
        scratch_shapes=[pltpu.VMEM((tm, d), BF16)],
        compiler_params=_params("parallel", "arbitrary"),
    )(x, ln, w, gain, *tabs)


def _idx_kernel(x_ref, ln_ref, w_ref, lg_ref, lb_ref, cq_ref, saq_ref, sbq_ref,
                ck_ref, sak_ref, sbk_ref, qi_ref, kw_ref):
    h = _rms_rows_bf16(x_ref, ln_ref)
    acc = jnp.dot(h, w_ref[...], preferred_element_type=F32)
    half = IDX_ROPE_DIM // 2
    cq = cq_ref[...]
    saq = saq_ref[...]
    sbq = sbq_ref[...]
    for c in range(IDX_Q // LANES):
        sl = slice(c * LANES, (c + 1) * LANES)
        qi_ref[:, sl] = _rope(acc[:, sl], cq, saq, sbq, half)
    a = acc[:, IDX_Q:IDX_Q + LANES]
    lane = lax.broadcasted_iota(jnp.int32, a.shape, 1)
    is_key = lane < IDX_DIM
    mu = jnp.sum(jnp.where(is_key, a, 0.0), axis=-1, keepdims=True) * (1.0 / IDX_DIM)
    xc = a - mu
    var = jnp.sum(jnp.where(is_key, xc * xc, 0.0), axis=-1, keepdims=True) * (1.0 / IDX_DIM)
    y = xc * lax.rsqrt(var + RMS_EPS) * lg_ref[...] + lb_ref[...]
    kn = jnp.where(is_key, y, a * (IDX_HEADS ** -0.5 * IDX_DIM ** -0.5))
    kw_ref[...] = _rope(kn, ck_ref[...], sak_ref[...], sbk_ref[...], half)


def _idx_proj(x, ln, w, lg, lb, tabs_q, tabs_k, tm):
    n, d = x.shape
    row = lambda i: (i, 0)
    fix = lambda i: (0, 0)
    tab_spec = pl.BlockSpec((tm, LANES), row)
    return pl.pallas_call(
        _idx_kernel,
        out_shape=(jax.ShapeDtypeStruct((n, IDX_Q), F32), jax.ShapeDtypeStruct((n, LANES), F32)),
        grid=(n // tm,),
        in_specs=[pl.BlockSpec((tm, d), row), pl.BlockSpec((1, d), fix),
                  pl.BlockSpec((d, IDX_PROJ_PAD), fix),
                  pl.BlockSpec((1, LANES), fix), pl.BlockSpec((1, LANES), fix)] + [tab_spec] * 6,
        out_specs=(pl.BlockSpec((tm, IDX_Q), row), pl.BlockSpec((tm, LANES), row)),
        compiler_params=_params("parallel"),
    )(x, ln, w, lg, lb, *tabs_q, *tabs_k)


def _oproj_kernel(o_ref, w_ref, x_ref, out_ref, ob_ref):
    @pl.when(pl.program_id(1) == 0)
    def _():
        ob_ref[...] = o_ref[...].astype(BF16)

    out_ref[...] = x_ref[...] + jnp.dot(ob_ref[...], w_ref[...], preferred_element_type=F32)


def _out_proj(o, w, x, tm):
    n, d = x.shape
    tn = min(QKV_COL_TILE, d)
    return pl.pallas_call(
        _oproj_kernel,
        out_shape=jax.ShapeDtypeStruct((n, d), F32),
        grid=(n // tm, d // tn),
        in_specs=[pl.BlockSpec((tm, Q_DIM), lambda i, j: (i, 0)),
                  pl.BlockSpec((Q_DIM, tn), lambda i, j: (0, j)),
                  pl.BlockSpec((tm, tn), lambda i, j: (i, j))],
        out_specs=pl.BlockSpec((tm, tn), lambda i, j: (i, j)),
        scratch_shapes=[pltpu.VMEM((tm, Q_DIM), BF16)],
        compiler_params=_params("parallel", "arbitrary"),
    )(o, w, x)


def _ffn_kernel(x_ref, ln_ref, wg_ref, wu_ref, wd_ref, o_ref, h_ref, acc_ref):
    f = pl.program_id(1)

    @pl.when(f == 0)
    def _():
        h_ref[...] = _rms_rows_bf16(x_ref, ln_ref)
        acc_ref[...] = jnp.zeros_like(acc_ref)

    h = h_ref[...]
    g = jnp.dot(h, wg_ref[...], preferred_element_type=F32)
    u = jnp.dot(h, wu_ref[...], preferred_element_type=F32)
    a = (g * jax.nn.sigmoid(g)) * u
    acc_ref[...] += jnp.dot(a.astype(BF16), wd_ref[...], preferred_element_type=F32)

    @pl.when(f == pl.num_programs(1) - 1)
    def _():
        o_ref[...] = x_ref[...] + acc_ref[...]


def _ffn(x, ln, w_gu, w_d, tm):
    n, d = x.shape
    d_ff = w_d.shape[0]
    tf = min(FFN_COL_TILE, d_ff)
    nf = d_ff // tf
    return pl.pallas_call(
        _ffn_kernel,
        out_shape=jax.ShapeDtypeStruct((n, d), F32),
        grid=(n // tm, nf),
        in_specs=[pl.BlockSpec((tm, d), lambda i, f: (i, 0)),
                  pl.BlockSpec((1, d), lambda i, f: (0, 0)),
                  pl.BlockSpec((d, tf), lambda i, f: (0, f)),
                  pl.BlockSpec((d, tf), lambda i, f: (0, f + nf)),
                  pl.BlockSpec((tf, d), lambda i, f: (f, 0))],
        out_specs=pl.BlockSpec((tm, d), lambda i, f: (i, 0)),
        scratch_shapes=[pltpu.VMEM((tm, d), BF16), pltpu.VMEM((tm, d), F32)],
        compiler_params=_params("parallel", "arbitrary"),
    )(x, ln, w_gu, w_gu, w_d)


def _sortable_key(score):
    bits = pltpu.bitcast(score, jnp.int32)
    return jnp.where(bits < 0, bits ^ jnp.int32(0x7FFFFFFF), bits)


def _kth_largest_key(key_ref, k, rows, n_chunks=None):
    width = key_ref.shape[2]

    def count_ge(cand):
        if n_chunks is None:
            ge = jnp.where(key_ref[...] >= cand[None], 1.0, 0.0)
            return jnp.sum(jnp.sum(ge, axis=0), axis=-1, keepdims=True)

        def add_chunk(c, part):
            ge = jnp.where(key_ref[c] >= cand, 1.0, 0.0)
            for w in range(width // LANES):
                part = part + ge[:, w * LANES:(w + 1) * LANES]
            return part

        part = lax.fori_loop(0, n_chunks, add_chunk, jnp.zeros((rows, LANES), F32))
        return jnp.sum(part, axis=-1, keepdims=True)

    zero = jnp.zeros((rows, 1), jnp.int32)
    t0 = jnp.where(count_ge(zero) >= k, zero, jnp.int32(INT32_MIN))

    def body(it, t):
        cand = t | lax.shift_left(jnp.int32(1), jnp.int32(30) - it)
        return jnp.where(count_ge(cand) >= k, cand, t)

    return lax.fori_loop(0, 31, body, t0)


def _split_bf16(x):
    hi = x.astype(BF16)
    return hi, (x - hi.astype(F32)).astype(BF16)


def _index_heads(qi, wq):
    return ([qi[:, h * IDX_DIM:(h + 1) * IDX_DIM].astype(BF16) for h in range(IDX_HEADS)],
            [wq[:, IDX_DIM + h:IDX_DIM + h + 1] for h in range(IDX_HEADS)])


def _index_scores(heads, ki):
    sc = None
    for qh, wh in zip(*heads):
        t = jnp.maximum(lax.dot_general(qh, ki, _NT, preferred_element_type=F32), 0.0) * wh
        sc = t if sc is None else sc + t
    return sc


def _top3_blocks(gate, own, n_blk):
    lane = lax.broadcasted_iota(jnp.int32, gate.shape, 1)
    lane_f = lane.astype(F32)
    gm = jnp.where(lane < own, gate, NEG_INF)
    gm = jnp.where(lane < n_blk, gm, -jnp.inf)
    sel = jnp.zeros(gate.shape, F32)
    for _ in range(MOBA_TOPK):
        mx = jnp.max(gm, axis=-1, keepdims=True)
        first = jnp.min(jnp.where(gm == mx, lane_f, float(LANES)), axis=-1, keepdims=True)
        pick = lane_f == first
        sel = jnp.where(pick, 1.0, sel)
        gm = jnp.where(pick, -jnp.inf, gm)
    return jnp.where(lane < own, sel, jnp.where(lane == own, 1.0, 0.0))


def _top3_blocks_t(gate_t, own, n_blk):
    sub = lax.broadcasted_iota(jnp.int32, gate_t.shape, 0)
    sub_f = sub.astype(F32)
    gm = jnp.where(sub < own, gate_t, NEG_INF)
    gm = jnp.where(sub < n_blk, gm, -jnp.inf)
    sel = jnp.zeros(gate_t.shape, F32)
    for _ in range(MOBA_TOPK):
        mx = jnp.max(gm, axis=0, keepdims=True)
        first = jnp.min(jnp.where(gm == mx, sub_f, float(LANES)), axis=0, keepdims=True)
        pick = sub_f == first
        sel = jnp.where(pick, 1.0, sel)
        gm = jnp.where(pick, -jnp.inf, gm)
    return jnp.where(sub < own, sel, jnp.where(sub == own, 1.0, 0.0))


def _softmax_probs(s, mask, m, l):
    for mk in (mask if isinstance(mask, tuple) else (mask,)):
        s = jnp.where(mk, s, NEG_INF)
    mn = jnp.maximum(m, jnp.max(s, axis=-1, keepdims=True))
    alpha = jnp.exp2(m - mn)
    p = jnp.exp2(s - mn)
    return p, mn, alpha * l + jnp.sum(p, axis=-1, keepdims=True), alpha


def _dsa_select_kernel(qi_ref, wq_ref, kk_ref, mask_ref, key_ref, *, topk):
    i = pl.program_id(1)
    nc, tq, kc = key_ref.shape
    qi = qi_ref[...]
    wq = wq_ref[...]
    qpos = i * tq + lax.broadcasted_iota(jnp.int32, (tq, 1), 0)
    col = lax.broadcasted_iota(jnp.int32, (tq, kc), 1)

    n_live = _div_pow2((i + 1) * tq + kc - 1, kc)
    heads = _index_heads(qi, wq)

    def chunk(c, carry):
        k0 = pl.multiple_of(c * kc, kc)
        ki = kk_ref[pl.ds(k0, kc), :][:, :IDX_DIM].astype(BF16)
        sc = jnp.where(col + k0 <= qpos, _index_scores(heads, ki), NEG_INF)
        key_ref[c] = _sortable_key(sc)
        return carry

    lax.fori_loop(0, n_live, chunk, 0)
    thr = _kth_largest_key(key_ref, topk, tq, n_chunks=n_live)

    def emit(c, carry):
        keep = jnp.where(key_ref[c] >= thr, 1.0, 0.0)
        mask_ref[c] = jnp.where(col + c * kc <= qpos, keep, 0.0).T.astype(BF16)
        return carry

    lax.fori_loop(0, n_live, emit, 0)

    def blank(c, carry):
        mask_ref[c] = jnp.zeros((kc, tq), BF16)
        return carry

    lax.fori_loop(n_live, nc, blank, 0)


def _dsa_select_prompt(qi, kw, b, t):
    tq, kc = ATTN_Q_TILE, DSA_KEY_CHUNK
    nq, nc = t // tq, t // kc
    topk = min(DSA_TOPK_MAX, t // 4)
    assert kc >= topk
    return pl.pallas_call(
        functools.partial(_dsa_select_kernel, topk=topk),
        out_shape=jax.ShapeDtypeStruct((b * nq, nc, kc, tq), BF16),
        grid=(b, nq),
        in_specs=[pl.BlockSpec((tq, IDX_Q), lambda bi, i: (bi * nq + i, 0)),
                  pl.BlockSpec((tq, LANES), lambda bi, i: (bi * nq + i, 0)),
                  pl.BlockSpec((t, LANES), lambda bi, i: (bi, 0))],
        out_specs=pl.BlockSpec((None, nc, kc, tq), lambda bi, i: (bi * nq + i, 0, 0, 0)),
        scratch_shapes=[pltpu.VMEM((nc, tq, kc), jnp.int32)],
        compiler_params=_params("parallel", "arbitrary"),
    )(qi, kw, kw)


def _prompt_attn_kernel(*refs, mode, t, kc):
    if mode == "swa":
        sink_ref, q_ref, k_ref, v_ref, o_ref = refs
    elif mode == "dsa":
        q_ref, k_ref, v_ref, mask_ref, o_ref = refs
    else:
        q_ref, k_ref, v_ref, o_ref, kmean_ref = refs
    i = pl.program_id(1)
    tq = q_ref.shape[0]
    rows = GROUP * tq
    qlane = lax.broadcasted_iota(jnp.int32, (1, rows), 1)
    qpos = i * tq + _mod_pow2(qlane, tq)
    ksub = lax.broadcasted_iota(jnp.int32, (kc, 1), 0)

    if mode == "moba":
        n_blk = t // MOBA_BLOCK
        n_blk8 = -(-n_blk // 8) * 8

        @pl.when(i == 0)
        def _():
            kmean_ref[...] = jnp.zeros_like(kmean_ref)
            for g in range(N_KV_HEADS):
                kg = k_ref[:, g * HEAD_DIM:(g + 1) * HEAD_DIM]
                ksum = jnp.sum(kg.reshape(n_blk, MOBA_BLOCK, HEAD_DIM), axis=1)
                kmean_ref[g, 0:n_blk, :] = ksum * (1.0 / MOBA_BLOCK)

    assert kc % tq == 0
    last = _div_pow2(i * tq, kc)
    first = jnp.maximum(last - 1, 0) if mode == "swa" else 0

    qbs, allows, init = [], [], []
    head = _div_pow2(qlane, tq)
    for g in range(N_KV_HEADS):
        qg = jnp.concatenate(
            [q_ref[:, (g * GROUP + n) * HEAD_DIM:(g * GROUP + n + 1) * HEAD_DIM] for n in range(GROUP)], axis=0)
        qbs.append((qg * (ATTN_SCALE * LOG2E)).astype(BF16))
        if mode == "moba":
            gate_t = lax.dot_general(kmean_ref[g], qg, _NT, precision=lax.Precision.HIGHEST,
                                     preferred_element_type=F32)[:n_blk8]
            allows.append(_top3_blocks_t(gate_t, _div_pow2(qpos, MOBA_BLOCK), n_blk))
        if mode == "swa":
            m0 = jnp.full((1, rows), sink_ref[g * GROUP + GROUP - 1], F32)
            for n in range(GROUP - 1):
                m0 = jnp.where(head == n, sink_ref[g * GROUP + n], m0)
            init += [m0 * LOG2E, jnp.ones((1, rows), F32)]
        else:
            init += [jnp.full((1, rows), NEG_INF, F32), jnp.zeros((1, rows), F32)]
        init.append(jnp.zeros((HEAD_DIM, rows), F32))

    def step(c, carry, diagonal):
        k0 = pl.multiple_of(c * kc, kc)
        kpos = ksub + k0
        if mode == "dsa":
            shared = (jnp.concatenate([mask_ref[c]] * GROUP, axis=1).astype(F32) > 0.5,)
        elif mode == "moba":
            shared = (kpos <= qpos,) if diagonal else ()
        elif diagonal:
            shared = (kpos <= qpos,)
        else:
            shared = (qpos - kpos < WINDOW,)
        out = []
        for g in range(N_KV_HEADS):
            m, l, acc = carry[3 * g:3 * g + 3]
            gsl = slice(g * HEAD_DIM, (g + 1) * HEAD_DIM)
            kb = k_ref[pl.ds(k0, kc), gsl].astype(BF16)
            vb = v_ref[pl.ds(k0, kc), gsl].astype(BF16)
            s = lax.dot_general(kb, qbs[g], _NT, preferred_element_type=F32)
            masks = shared
            if mode == "moba":
                blk_sub = lax.broadcasted_iota(jnp.int32, allows[g].shape, 0)
                picked = jnp.max(jnp.where(blk_sub == c, allows[g], 0.0), axis=0, keepdims=True) > 0.5
                masks = (picked,) + shared
            for mk in masks:
                s = jnp.where(mk, s, NEG_INF)
            mn = jnp.maximum(m, jnp.max(s, axis=0, keepdims=True))
            alpha = jnp.exp2(m - mn)
            p = jnp.exp2(s - mn)
            l = alpha * l + jnp.sum(p, axis=0, keepdims=True)
            pv = lax.dot_general(vb, p.astype(BF16), _TN, preferred_element_type=F32)
            out += [mn, l, alpha * acc + pv]
        return tuple(out)

    carry = lax.fori_loop(first, last, functools.partial(step, diagonal=False), tuple(init))
    carry = step(last, carry, True)
    for g in range(N_KV_HEADS):
        o_t = carry[3 * g + 2] / carry[3 * g + 1]
        for n in range(GROUP):
            hsl = slice((g * GROUP + n) * HEAD_DIM, (g * GROUP + n + 1) * HEAD_DIM)
            o_ref[:, hsl] = o_t[:, n * tq:(n + 1) * tq].T


def _prompt_attn(qkv, b, t, mode, mask=None, sinks=None):
    tq = ATTN_Q_TILE
    nq = t // tq
    kc = {"dsa": DSA_KEY_CHUNK, "moba": MOBA_BLOCK, "swa": WINDOW}[mode]
    in_specs = [pl.BlockSpec((tq, Q_DIM), lambda bi, i: (bi * nq + i, 0)),
                pl.BlockSpec((t, KV_DIM), lambda bi, i: (bi, Q_DIM // KV_DIM)),
                pl.BlockSpec((t, KV_DIM), lambda bi, i: (bi, Q_DIM // KV_DIM + 1))]
    args = [qkv, qkv, qkv]
    scratch = []
    if mode == "dsa":
        nc = t // kc
        in_specs.append(pl.BlockSpec((None, nc, kc, tq), lambda bi, i: (bi * nq + i, 0, 0, 0)))
        args.append(mask)
    elif mode == "moba":
        assert t % MOBA_BLOCK == 0 and t // MOBA_BLOCK <= LANES
        scratch.append(pltpu.VMEM((N_KV_HEADS, LANES, HEAD_DIM), F32))
    else:
        assert tq == WINDOW
        in_specs.insert(0, pl.BlockSpec(memory_space=pltpu.SMEM))
        args.insert(0, sinks)
    return pl.pallas_call(
        functools.partial(_prompt_attn_kernel, mode=mode, t=t, kc=kc),
        out_shape=jax.ShapeDtypeStruct((b * t, Q_DIM), F32),
        grid=(b, nq),
        in_specs=in_specs,
        out_specs=pl.BlockSpec((tq, Q_DIM), lambda bi, i: (bi * nq + i, 0)),
        scratch_shapes=scratch,
        compiler_params=_params("parallel", "arbitrary"),
    )(*args)


def _page_specs(shape_tail, n_pages, pages_per_step):
    def spec(j):
        def index(bi, c, pt):
            page = jnp.minimum(c * pages_per_step + j, n_pages - 1)
            return (pt[bi * n_pages + page],) + (0,) * len(shape_tail)
        return pl.BlockSpec((None,) + shape_tail, index)
    return [spec(j) for j in range(pages_per_step)]


def _dsa_sample_select_kernel(pt_ref, qi_ref, w_ref, *refs, pages, topk, s_len):
    ik_refs = refs[:pages]
    knew_ref, mask_ref, key_ref = refs[pages:]
    c = pl.program_id(1)
    last = pl.num_programs(1) - 1
    nsteps, srows, width = key_ref.shape
    rows = mask_ref.shape[1]
    qi = qi_ref[...]
    w = w_ref[...]

    q_hi, q_lo = _split_bf16(qi)

    def scores(ki_t):
        k_hi, k_lo = _split_bf16(ki_t)
        d = (jnp.dot(q_hi, k_hi, preferred_element_type=F32) + jnp.dot(q_hi, k_lo, preferred_element_type=F32)
             + jnp.dot(q_lo, k_hi, preferred_element_type=F32))
        r = jnp.maximum(d, 0.0) * w
        srow = lax.broadcasted_iota(jnp.int32, (srows, ki_t.shape[1]), 0)
        sc = jnp.zeros((srows, ki_t.shape[1]), F32)
        for s in range(s_len):
            tot = jnp.sum(r[s * IDX_HEADS:(s + 1) * IDX_HEADS], axis=0, keepdims=True)
            sc = jnp.where(srow == s, tot, sc)
        return sc

    srow_w = lax.broadcasted_iota(jnp.int32, (srows, width), 0)
    new_ok = (lax.broadcasted_iota(jnp.int32, (srows, width), 1) <= srow_w) & (srow_w < s_len)

    @pl.when(c < last)
    def _():
        key_ref[c] = _sortable_key(scores(jnp.concatenate([r[...] for r in ik_refs], axis=1)))

    @pl.when(c == last)
    def _():
        sc = jnp.concatenate([scores(knew_ref[...]), jnp.zeros((srows, width - PAGE_SIZE), F32)], axis=1)
        key_ref[c] = _sortable_key(jnp.where(new_ok, sc, NEG_INF))
        thr = _kth_largest_key(key_ref, topk, srows)
        tok = _mod_pow2(lax.broadcasted_iota(jnp.int32, (rows, width), 0), s_len)
        for cc in range(nsteps):
            keep = key_ref[cc] >= thr
            if cc == nsteps - 1:
                keep = keep & new_ok
            keep = jnp.where(keep, 1.0, 0.0)
            out = jnp.zeros((rows, width), F32)
            for s in range(s_len):
                out = jnp.where(tok == s, keep[s:s + 1, :], out)
            mask_ref[0, :, cc * width:(cc + 1) * width] = out.astype(BF16)


def _dsa_select_sample(pt_flat, qi_s, w_s, cache_ik_t, ki_new_t, n_pages, s_len):
    db = qi_s.shape[0]
    rows = GROUP * s_len
    srows = 8
    pages = min(PAGES_PER_STEP, n_pages)
    assert n_pages % pages == 0 and s_len <= srows and pages > 1
    nsteps = n_pages // pages + 1
    width = pages * PAGE_SIZE
    lp = nsteps * width
    bsel = lambda bi, c, pt: (bi, 0, 0)
    grid_spec = pltpu.PrefetchScalarGridSpec(
        num_scalar_prefetch=1,
        grid=(db, nsteps),
        in_specs=[pl.BlockSpec((None, s_len * IDX_HEADS, IDX_DIM), bsel),
                  pl.BlockSpec((None, s_len * IDX_HEADS, 1), bsel)]
        + _page_specs((IDX_DIM, PAGE_SIZE), n_pages, pages)
        + [pl.BlockSpec((None, IDX_DIM, PAGE_SIZE), bsel)],
        out_specs=pl.BlockSpec((None, 1, rows, lp), lambda bi, c, pt: (bi, 0, 0, 0)),
        scratch_shapes=[pltpu.VMEM((nsteps, srows, width), jnp.int32)],
    )
    topk = min(DSA_TOPK_MAX, (n_pages * PAGE_SIZE + s_len) // 4)
    return pl.pallas_call(
        functools.partial(_dsa_sample_select_kernel, pages=pages, topk=topk, s_len=s_len),
        out_shape=jax.ShapeDtypeStruct((db, 1, rows, lp), BF16),
        grid_spec=grid_spec,
        compiler_params=_params("parallel", "arbitrary"),
    )(pt_flat, qi_s, w_s, *([cache_ik_t] * pages), ki_new_t)


def _moba_block_mean_kernel(pt_ref, *refs, pages):
    k_refs = refs[:pages]
    o_ref = refs[pages]
    per_blk = MOBA_BLOCK // PAGE_SIZE
    for j in range(pages // per_blk):
        tot = None
        for u in range(per_blk):
            page = k_refs[j * per_blk + u][...]
            part = jnp.sum(page.reshape(page.shape[0] // 8, 8, HEAD_DIM), axis=0)
            tot = part if tot is None else tot + part
        o_ref[j * 8:(j + 1) * 8, :] = tot * (1.0 / MOBA_BLOCK)


def _moba_block_means(pt_flat, cache_k, db, n_pages):
    pages = MOBA_SUM_PAGES
    assert n_pages % pages == 0 and 2 * N_KV_HEADS == 8
    per_step = pages * PAGE_SIZE // MOBA_BLOCK
    grid_spec = pltpu.PrefetchScalarGridSpec(
        num_scalar_prefetch=1,
        grid=(db, n_pages // pages),
        in_specs=_page_specs((PAGE_SIZE * N_KV_HEADS, HEAD_DIM), n_pages, pages),
        out_specs=pl.BlockSpec((None, per_step * 8, HEAD_DIM), lambda bi, c, pt: (bi, c, 0)),
    )
    return pl.pallas_call(
        functools.partial(_moba_block_mean_kernel, pages=pages),
        out_shape=jax.ShapeDtypeStruct((db, n_pages * PAGE_SIZE // MOBA_BLOCK * 8, HEAD_DIM), F32),
        grid_spec=grid_spec,
        compiler_params=_params("parallel", "arbitrary"),
    )(pt_flat, *([cache_k] * pages))


def _moba_sample_gate_kernel(q_ref, kmean_ref, allow_ref, *, n_past_blk, n_blk, s_len, past_len):
    rows = q_ref.shape[1]
    tok = _mod_pow2(lax.broadcasted_iota(jnp.int32, (rows, 1), 0), s_len)
    own = _div_pow2(past_len + tok, MOBA_BLOCK)
    for g in range(N_KV_HEADS):
        km = (kmean_ref[pl.ds(g, n_past_blk, stride=8), :]
              + kmean_ref[pl.ds(g + N_KV_HEADS, n_past_blk, stride=8), :])
        km = jnp.concatenate([km, jnp.zeros((LANES - n_past_blk, HEAD_DIM), F32)], axis=0)
        gate = lax.dot_general(q_ref[g], km, _NT, precision=lax.Precision.HIGHEST, preferred_element_type=F32)
        allow_ref[g] = _top3_blocks(gate, own, n_blk)


def _moba_sample_gate(qs, kmean, n_pages, s_len):
    db, _, rows, _ = qs.shape
    past_len = n_pages * PAGE_SIZE
    n_past_blk = kmean.shape[1] // 8
    n_blk = -(-(past_len + s_len) // MOBA_BLOCK)
    assert past_len % MOBA_BLOCK == 0 and s_len <= MOBA_BLOCK and n_blk <= LANES
    return pl.pallas_call(
        functools.partial(_moba_sample_gate_kernel, n_past_blk=n_past_blk, n_blk=n_blk, s_len=s_len,
                          past_len=past_len),
        out_shape=jax.ShapeDtypeStruct((db, N_KV_HEADS, rows, LANES), F32),
        grid=(db,),
        in_specs=[pl.BlockSpec((None, N_KV_HEADS, rows, HEAD_DIM), lambda bi: (bi, 0, 0, 0)),
                  pl.BlockSpec((None, n_past_blk * 8, HEAD_DIM), lambda bi: (bi, 0, 0))],
        out_specs=pl.BlockSpec((None, N_KV_HEADS, rows, LANES), lambda bi: (bi, 0, 0, 0)),
        compiler_params=_params("parallel"),
    )(qs, kmean)


def _paged_attn_kernel(pt_ref, *refs, pages, mode, use_sink, s_len, past_len):
    if use_sink:
        sink_ref, refs = refs[0], refs[1:]
    q_ref = refs[0]
    k_refs = refs[1:1 + pages]
    v_refs = refs[1 + pages:1 + 2 * pages]
    knew_ref, vnew_ref, sel_ref, o_ref, m_ref, l_ref, acc_ref = refs[1 + 2 * pages:]
    c = pl.program_id(1)
    last = pl.num_programs(1) - 1
    rows = q_ref.shape[1]
    all_rows = N_KV_HEADS * rows
    row = lax.broadcasted_iota(jnp.int32, (all_rows, 1), 0)
    tok = _mod_pow2(row, s_len)

    @pl.when(c == 0)
    def _():
        if use_sink:
            head = _div_pow2(row, s_len)
            m0 = jnp.full((all_rows, 1), sink_ref[N_HEADS - 1], F32)
            for h in range(N_HEADS - 1):
                m0 = jnp.where(head == h, sink_ref[h], m0)
            m_ref[...] = m0 * LOG2E
            l_ref[...] = jnp.ones((all_rows, 1), F32)
        else:
            m_ref[...] = jnp.full((all_rows, 1), NEG_INF, F32)
            l_ref[...] = jnp.zeros((all_rows, 1), F32)
        acc_ref[...] = jnp.zeros((all_rows, HEAD_DIM), F32)

    def block_allowed(blk):
        lane = lax.broadcasted_iota(jnp.int32, (all_rows, LANES), 1)
        allow = sel_ref[...].reshape(all_rows, LANES)
        return jnp.max(jnp.where(lane == blk, allow, 0.0), axis=-1, keepdims=True) > 0.5

    def head_rows(page_refs, g):
        return jnp.concatenate([r[pl.ds(g, PAGE_SIZE, stride=N_KV_HEADS), :] for r in page_refs],
                               axis=0).astype(BF16)

    def update(k_pages, v_pages, mask):
        qb = (q_ref[...].reshape(all_rows, HEAD_DIM) * (ATTN_SCALE * LOG2E)).astype(BF16)
        s = jnp.concatenate(
            [lax.dot_general(qb[g * rows:(g + 1) * rows], head_rows(k_pages, g), _NT, preferred_element_type=F32)
             for g in range(N_KV_HEADS)], axis=0)
        p, m, l, alpha = _softmax_probs(s, mask, m_ref[...], l_ref[...])
        pb = p.astype(BF16)
        pv = jnp.concatenate(
            [jnp.dot(pb[g * rows:(g + 1) * rows], head_rows(v_pages, g), preferred_element_type=F32)
             for g in range(N_KV_HEADS)], axis=0)
        m_ref[...] = m
        l_ref[...] = l
        acc_ref[...] = alpha * acc_ref[...] + pv

    @pl.when(c < last)
    def _():
        if mode == "mask":
            mask = jnp.concatenate([sel_ref[0]] * N_KV_HEADS, axis=0).astype(F32) > 0.5
        else:
            mask = jnp.concatenate(
                [jnp.broadcast_to(block_allowed(_div_pow2(c * pages + j, MOBA_BLOCK // PAGE_SIZE)),
                                  (all_rows, PAGE_SIZE)) for j in range(pages)], axis=1)
        update(k_refs, v_refs, mask)

    @pl.when(c == last)
    def _():
        if mode == "mask":
            mask = jnp.concatenate([sel_ref[0][:, :PAGE_SIZE]] * N_KV_HEADS, axis=0).astype(F32) > 0.5
        else:
            col = lax.broadcasted_iota(jnp.int32, (all_rows, PAGE_SIZE), 1)
            mask = (block_allowed(past_len // MOBA_BLOCK), col <= tok)
        update([knew_ref], [vnew_ref], mask)
        o_ref[...] = (acc_ref[...] / l_ref[...]).reshape(N_KV_HEADS, rows, HEAD_DIM)


def _paged_attn(pt_flat, qs, cache_k, cache_v, k_new, v_new, sel, n_pages, s_len, mode, sinks=None):
    db, _, rows, _ = qs.shape
    pages = min(PAGES_PER_STEP, n_pages)
    assert n_pages % pages == 0
    nsteps = n_pages // pages + 1
    width = pages * PAGE_SIZE
    bsel3 = lambda bi, c, pt: (bi, 0, 0)
    bsel4 = lambda bi, c, pt: (bi, 0, 0, 0)
    if mode == "mask":
        sel_spec = pl.BlockSpec((None, 1, rows, width), lambda bi, c, pt: (bi, 0, 0, c))
    else:
        assert (n_pages * PAGE_SIZE) % MOBA_BLOCK == 0 and s_len <= MOBA_BLOCK
        sel_spec = pl.BlockSpec((None, N_KV_HEADS, rows, LANES), bsel4)
    page_shape = (PAGE_SIZE * N_KV_HEADS, HEAD_DIM)
    in_specs = ([pl.BlockSpec((None, N_KV_HEADS, rows, HEAD_DIM), bsel4)]
                + _page_specs(page_shape, n_pages, pages) + _page_specs(page_shape, n_pages, pages)
                + [pl.BlockSpec((None,) + page_shape, bsel3)] * 2 + [sel_spec])
    args = [qs] + [cache_k] * pages + [cache_v] * pages + [k_new, v_new, sel]
    if sinks is not None:
        in_specs.insert(0, pl.BlockSpec(memory_space=pltpu.SMEM))
        args.insert(0, sinks)
    grid_spec = pltpu.PrefetchScalarGridSpec(
        num_scalar_prefetch=1,
        grid=(db, nsteps),
        in_specs=in_specs,
        out_specs=pl.BlockSpec((None, N_KV_HEADS, rows, HEAD_DIM), bsel4),
        scratch_shapes=[pltpu.VMEM((N_KV_HEADS * rows, 1), F32), pltpu.VMEM((N_KV_HEADS * rows, 1), F32),
                        pltpu.VMEM((N_KV_HEADS * rows, HEAD_DIM), F32)],
    )
    return pl.pallas_call(
        functools.partial(_paged_attn_kernel, pages=pages, mode=mode, use_sink=sinks is not None,
                          s_len=s_len, past_len=n_pages * PAGE_SIZE),
        out_shape=jax.ShapeDtypeStruct((db, N_KV_HEADS, rows, HEAD_DIM), F32),
        grid_spec=grid_spec,
        compiler_params=_params("parallel", "arbitrary"),
    )(pt_flat, *args)


def _pad_rows(a, rows):
    return jnp.pad(a, ((0, 0), (0, rows - a.shape[1]), (0, 0)))


def _as_pages(cache):
    assert cache.shape[1:] == (PAGE_SIZE, N_KV_HEADS, HEAD_DIM), cache.shape
    return cache.reshape(cache.shape[0], PAGE_SIZE * N_KV_HEADS, HEAD_DIM)


def kernel(x_prompt, x_sample, cache_k_0, cache_v_0, cache_ik_0, cache_k_1, cache_v_1, state_wk_2, state_wv_2,
           cache_k_3, cache_v_3, cache_ik_3, page_table, ln_attn, w_qkv, q_norm, k_norm, w_o, w_idx, idx_ln_g,
           idx_ln_b, sinks, ln_ffn, w_gate_up, w_down):
    b, t, d = x_prompt.shape
    db, s_len, _ = x_sample.shape
    n_pages = page_table.shape[1]
    past_len = n_pages * PAGE_SIZE
    depth = w_qkv.shape[0]
    n_p = b * t
    n = n_p + db * s_len
    tm = _row_tile(n)
    rows = GROUP * s_len
    assert t % DSA_KEY_CHUNK == 0 and t % ATTN_Q_TILE == 0 and rows % 8 == 0

    caches = {0: (cache_k_0, cache_v_0, cache_ik_0), 1: (cache_k_1, cache_v_1), 2: (state_wk_2, state_wv_2),
              3: (cache_k_3, cache_v_3, cache_ik_3)}
    pt_flat = page_table.reshape(-1).astype(jnp.int32)
    win_pt = jnp.arange(db, dtype=jnp.int32)

    pos = jnp.concatenate([jnp.tile(jnp.arange(t, dtype=jnp.int32), b),
                           jnp.tile(past_len + jnp.arange(s_len, dtype=jnp.int32), db)])
    tabs_main = _rope_tables(pos, ROPE_DIM // 2, HEAD_DIM)
    tabs_iq = _rope_tables(pos, IDX_ROPE_DIM // 2, IDX_DIM)
    tabs_ik = _rope_tables(pos, IDX_ROPE_DIM // 2, LANES)

    tok = (jnp.arange(rows, dtype=jnp.int32) % s_len)[:, None]
    col = jnp.arange(PAGE_SIZE, dtype=jnp.int32)[None, :]
    swa_mask = jnp.concatenate([col > tok + (WINDOW - PAGE_SIZE), col <= tok], axis=1)
    swa_mask = jnp.broadcast_to(swa_mask.astype(BF16)[None, None], (db, 1, rows, 2 * PAGE_SIZE))

    x = jnp.concatenate([x_prompt.reshape(n_p, d), x_sample.reshape(db * s_len, d)], axis=0)
    outs = {}
    for i in range(depth):
        kind, j = i % 3, i // 3
        gain = jnp.concatenate([jnp.tile(q_norm[i], N_HEADS), jnp.tile(k_norm[i], N_KV_HEADS),
                                jnp.ones((KV_DIM,), F32)])[None, :]
        ln_a = ln_attn[i][None, :]
        qkv = _qkv_proj(x, ln_a, w_qkv[i].astype(BF16), gain, tabs_main, tm)
        k_all = qkv[:, Q_DIM:Q_DIM + KV_DIM]
        v_all = qkv[:, Q_DIM + KV_DIM:]
        k_p = k_all[:n_p].reshape(b, t, N_KV_HEADS, HEAD_DIM)
        v_p = v_all[:n_p].reshape(b, t, N_KV_HEADS, HEAD_DIM)
        k_s = k_all[n_p:].reshape(db, s_len, N_KV_HEADS, HEAD_DIM)
        v_s = v_all[n_p:].reshape(db, s_len, N_KV_HEADS, HEAD_DIM)
        k_new = _pad_rows(k_all[n_p:].reshape(db, s_len * N_KV_HEADS, HEAD_DIM), PAGE_SIZE * N_KV_HEADS)
        v_new = _pad_rows(v_all[n_p:].reshape(db, s_len * N_KV_HEADS, HEAD_DIM), PAGE_SIZE * N_KV_HEADS)
        qs = qkv[n_p:, :Q_DIM].reshape(db, s_len, N_KV_HEADS, GROUP, HEAD_DIM)
        qs = qs.transpose(0, 2, 3, 1, 4).reshape(db, N_KV_HEADS, rows, HEAD_DIM)

        if kind == 0:
            ck, cv, cik = caches[i]
            w_i = jnp.pad(w_idx[j], ((0, 0), (0, IDX_PROJ_PAD - IDX_PROJ))).astype(BF16)
            lg = jnp.pad(idx_ln_g[j], (0, LANES - IDX_DIM))[None, :]
            lb = jnp.pad(idx_ln_b[j], (0, LANES - IDX_DIM))[None, :]
            qi, kw = _idx_proj(x, ln_a, w_i, lg, lb, tabs_iq, tabs_ik, tm)
            mask_p = _dsa_select_prompt(qi, kw, b, t)
            o_p = _prompt_attn(qkv, b, t, "dsa", mask=mask_p)
            qi_s = qi[n_p:].reshape(db, s_len * IDX_HEADS, IDX_DIM)
            w_s = kw[n_p:, IDX_DIM:IDX_DIM + IDX_HEADS].reshape(db, s_len * IDX_HEADS, 1)
            ki_new_t = _pad_rows(kw[n_p:, :IDX_DIM].reshape(db, s_len, IDX_DIM), PAGE_SIZE).transpose(0, 2, 1)
            mask_s = _dsa_select_sample(pt_flat, qi_s, w_s, cik.transpose(0, 2, 1), ki_new_t, n_pages, s_len)
            o_s = _paged_attn(pt_flat, qs, _as_pages(ck), _as_pages(cv), k_new, v_new, mask_s, n_pages, s_len,
                              "mask")
            ik_p = kw[:n_p, :IDX_DIM].reshape(b, t, IDX_DIM)
            ik_s = kw[n_p:, :IDX_DIM].reshape(db, s_len, IDX_DIM)
            outs[i] = (k_p, v_p, ik_p, k_s, v_s, ik_s)
        elif kind == 1:
            ck, cv = caches[i]
            ck2 = _as_pages(ck)
            cv2 = _as_pages(cv)
            o_p = _prompt_attn(qkv, b, t, "moba")
            kmean = _moba_block_means(pt_flat, ck2, db, n_pages)
            allow = _moba_sample_gate(qs, kmean, n_pages, s_len)
            o_s = _paged_attn(pt_flat, qs, ck2, cv2, k_new, v_new, allow, n_pages, s_len, "block")
            outs[i] = (k_p, v_p, k_s, v_s)
        else:
            bk, bv = caches[i]
            o_p = _prompt_attn(qkv, b, t, "swa", sinks=sinks[j])
            o_s = _paged_attn(win_pt, qs, _as_pages(bk), _as_pages(bv), k_new, v_new, swa_mask, 1, s_len, "mask",
                              sinks=sinks[j])
            outs[i] = (k_p[:, -WINDOW:], v_p[:, -WINDOW:],
                       jnp.concatenate([bk, k_s], axis=1)[:, -WINDOW:],
                       jnp.concatenate([bv, v_s], axis=1)[:, -WINDOW:])

        o_s = o_s.reshape(db, N_KV_HEADS, GROUP, s_len, HEAD_DIM).transpose(0, 3, 1, 2, 4).reshape(db * s_len, Q_DIM)
        o = jnp.concatenate([o_p, o_s], axis=0)
        x = _out_proj(o, w_o[i].astype(BF16), x, tm)
        x = _ffn(x, ln_ffn[i][None, :], w_gate_up[i].astype(BF16), w_down[i].astype(BF16), tm)

    y_p = x[:n_p].reshape(b, t, d)
    y_s = x[n_p:].reshape(db, s_len, d)
    return (y_p, y_s) + outs[0] + outs[1] + outs[2] + outs[3]
```

```python
import functools

import jax
import jax.numpy as jnp
from jax import lax
from jax.experimental import pallas as pl
from jax.experimental.pallas import tpu as pltpu

F32 = jnp.float32
BF16 = jnp.bfloat16

N_HEADS = 16
N_KV_HEADS = 4
HEAD_DIM = 128
GROUP = N_HEADS // N_KV_HEADS
Q_DIM = N_HEADS * HEAD_DIM
KV_DIM = N_KV_HEADS * HEAD_DIM
QKV_DIM = Q_DIM + 2 * KV_DIM
ROPE_THETA = 500000.0
ROPE_DIM = HEAD_DIM // 4
IDX_HEADS = 16
IDX_DIM = 64
IDX_ROPE_DIM = IDX_DIM // 4
IDX_Q = IDX_HEADS * IDX_DIM
IDX_PROJ = IDX_Q + IDX_DIM + IDX_HEADS
IDX_PROJ_PAD = IDX_Q + 128
DSA_TOPK_MAX = 256
MOBA_BLOCK = 256
MOBA_TOPK = 3
WINDOW = 128
PAGE_SIZE = 128
RMS_EPS = 1e-6
NEG_INF = -1e30
ATTN_SCALE = HEAD_DIM ** -0.5
INT32_MIN = -(2 ** 31)

LANES = 128
VMEM_LIMIT = 56 * 1024 * 1024
ROW_TILE_MAX = 640
LOG2E = 1.4426950408889634
QKV_COL_TILE = 512
FFN_COL_TILE = 512
ATTN_Q_TILE = 128
DSA_KEY_CHUNK = 512
PAGES_PER_STEP = 32
MOBA_SUM_PAGES = 16

_NT = (((1,), (1,)), ((), ()))
_TN = (((0,), (0,)), ((), ()))


def _params(*sem):
    return pltpu.CompilerParams(dimension_semantics=sem, vmem_limit_bytes=VMEM_LIMIT)


def _log2(v):
    assert v > 0 and v & (v - 1) == 0, v
    return v.bit_length() - 1


def _div_pow2(x, d):
    return x >> _log2(d)


def _mod_pow2(x, d):
    _log2(d)
    return x & (d - 1)


def _row_tile(n):
    best = None
    for t in range(LANES, ROW_TILE_MAX + 1, LANES):
        if n % t == 0:
            best = t
    assert best is not None, n
    return best


def _rope_tables(pos, half, period):
    inv_freq = ROPE_THETA ** (-jnp.arange(half, dtype=F32) / half)
    ang = pos.astype(F32)[:, None] * inv_freq[None, :]
    cos = jnp.cos(ang)
    sin = jnp.sin(ang)
    r = jnp.arange(LANES) % period
    lo = r < half
    hi = (r >= half) & (r < 2 * half)
    idx = jnp.clip(jnp.where(lo, r, r - half), 0, half - 1)
    cg = cos[:, idx]
    sg = sin[:, idx]
    return (jnp.where(lo | hi, cg, 1.0).astype(F32),
            jnp.where(lo, -sg, 0.0).astype(F32),
            jnp.where(hi, sg, 0.0).astype(F32))


def _rope(y, cos, sin_a, sin_b, half):
    return y * cos + pltpu.roll(y, LANES - half, 1) * sin_a + pltpu.roll(y, half, 1) * sin_b


def _rms_rows_bf16(x_ref, ln_ref):
    xf = x_ref[...]
    ms = jnp.mean(xf * xf, axis=-1, keepdims=True)
    return (xf * lax.rsqrt(ms + RMS_EPS) * ln_ref[...]).astype(BF16)


def _qkv_kernel(x_ref, ln_ref, w_ref, g_ref, cos_ref, sa_ref, sb_ref, o_ref, h_ref, *, n_norm_cols):
    j = pl.program_id(1)

    @pl.when(j == 0)
    def _():
        h_ref[...] = _rms_rows_bf16(x_ref, ln_ref)

    tn = o_ref.shape[1]
    acc = jnp.dot(h_ref[...], w_ref[...], preferred_element_type=F32)

    @pl.when((j + 1) * tn <= n_norm_cols)
    def _():
        cos = cos_ref[...]
        sa = sa_ref[...]
        sb = sb_ref[...]
        for hh in range(tn // HEAD_DIM):
            sl = slice(hh * HEAD_DIM, (hh + 1) * HEAD_DIM)
            a = acc[:, sl]
            ms = jnp.mean(a * a, axis=-1, keepdims=True)
            y = a * lax.rsqrt(ms + RMS_EPS) * g_ref[:, sl]
            o_ref[:, sl] = _rope(y, cos, sa, sb, ROPE_DIM // 2)

    @pl.when(j * tn >= n_norm_cols)
    def _():
        o_ref[...] = acc


def _qkv_proj(x, ln, w, gain, tabs, tm):
    n, d = x.shape
    tn = QKV_COL_TILE
    tab_spec = pl.BlockSpec((tm, LANES), lambda i, j: (i, 0))
    return pl.pallas_call(
        functools.partial(_qkv_kernel, n_norm_cols=Q_DIM + KV_DIM),
        out_shape=jax.ShapeDtypeStruct((n, QKV_DIM), F32),
        grid=(n // tm, QKV_DIM // tn),
        in_specs=[pl.BlockSpec((tm, d), lambda i, j: (i, 0)),
                  pl.BlockSpec((1, d), lambda i, j: (0, 0)),
                  pl.BlockSpec((d, tn), lambda i, j: (0, j)),
                  pl.BlockSpec((1, tn), lambda i, j: (0, j)),
                  tab_spec, tab_spec, tab_spec],
        out_specs=pl.BlockSpec((tm, tn), lambda i, j: (i, j)),
        scratch_shapes=[pltpu.VMEM((tm, d), BF16)],
        compiler_params=_params("parallel", "arbitrary"),
    )(x, ln, w, gain, *tabs)


def _idx_kernel(x_ref, ln_ref, w_ref, lg_ref, lb_ref, cq_ref, saq_ref, sbq_ref,
                ck_ref, sak_ref, sbk_ref, qi_ref, kw_ref):
    h = _rms_rows_bf16(x_ref, ln_ref)
    acc = jnp.dot(h, w_ref[...], preferred_element_type=F32)
    half = IDX_ROPE_DIM // 2
    cq = cq_ref[...]
    saq = saq_ref[...]
    sbq = sbq_ref[...]
    for c in range(IDX_Q // LANES):
        sl = slice(c * LANES, (c + 1) * LANES)
        qi_ref[:, sl] = _rope(acc[:, sl], cq, saq, sbq, half)
    a = acc[:, IDX_Q:IDX_Q + LANES]
    lane = lax.broadcasted_iota(jnp.int32, a.shape, 1)
    is_key = lane < IDX_DIM
    mu = jnp.sum(jnp.where(is_key, a, 0.0), axis=-1, keepdims=True) * (1.0 / IDX_DIM)
    xc = a - mu
    var = jnp.sum(jnp.where(is_key, xc * xc, 0.0), axis=-1, keepdims=True) * (1.0 / IDX_DIM)
    y = xc * lax.rsqrt(var + RMS_EPS) * lg_ref[...] + lb_ref[...]
    kn = jnp.where(is_key, y, a * (IDX_HEADS ** -0.5 * IDX_DIM ** -0.5))
    kw_ref[...] = _rope(kn, ck_ref[...], sak_ref[...], sbk_ref[...], half)


def _idx_proj(x, ln, w, lg, lb, tabs_q, tabs_k, tm):
    n, d = x.shape
    row = lambda i: (i, 0)
    fix = lambda i: (0, 0)
    tab_spec = pl.BlockSpec((tm, LANES), row)
    return pl.pallas_call(
        _idx_kernel,
        out_shape=(jax.ShapeDtypeStruct((n, IDX_Q), F32), jax.ShapeDtypeStruct((n, LANES), F32)),
        grid=(n // tm,),
        in_specs=[pl.BlockSpec((tm, d), row), pl.BlockSpec((1, d), fix),
                  pl.BlockSpec((d, IDX_PROJ_PAD), fix),
                  pl.BlockSpec((1, LANES), fix), pl.BlockSpec((1, LANES), fix)] + [tab_spec] * 6,
        out_specs=(pl.BlockSpec((tm, IDX_Q), row), pl.BlockSpec((tm, LANES), row)),
        compiler_params=_params("parallel"),
    )(x, ln, w, lg, lb, *tabs_q, *tabs_k)


def _oproj_kernel(o_ref, w_ref, x_ref, out_ref, ob_ref):
    @pl.when(pl.program_id(1) == 0)
    def _():
        ob_ref[...] = o_ref[...].astype(BF16)

    out_ref[...] = x_ref[...] + jnp.dot(ob_ref[...], w_ref[...], preferred_element_type=F32)


def _out_proj(o, w, x, tm):
    n, d = x.shape
    tn = min(QKV_COL_TILE, d)
    return pl.pallas_call(
        _oproj_kernel,
        out_shape=jax.ShapeDtypeStruct((n, d), F32),
        grid=(n // tm, d // tn),
        in_specs=[pl.BlockSpec((tm, Q_DIM), lambda i, j: (i, 0)),
                  pl.BlockSpec((Q_DIM, tn), lambda i, j: (0, j)),
                  pl.BlockSpec((tm, tn), lambda i, j: (i, j))],
        out_specs=pl.BlockSpec((tm, tn), lambda i, j: (i, j)),
        scratch_shapes=[pltpu.VMEM((tm, Q_DIM), BF16)],
        compiler_params=_params("parallel", "arbitrary"),
    )(o, w, x)


def _ffn_kernel(x_ref, ln_ref, wg_ref, wu_ref, wd_ref, o_ref, h_ref, acc_ref):
    f = pl.program_id(1)

    @pl.when(f == 0)
    def _():
        h_ref[...] = _rms_rows_bf16(x_ref, ln_ref)
        acc_ref[...] = jnp.zeros_like(acc_ref)

    h = h_ref[...]
    g = jnp.dot(h, wg_ref[...], preferred_element_type=F32)
    u = jnp.dot(h, wu_ref[...], preferred_element_type=F32)
    a = (g * jax.nn.sigmoid(g)) * u
    acc_ref[...] += jnp.dot(a.astype(BF16), wd_ref[...], preferred_element_type=F32)

    @pl.when(f == pl.num_programs(1) - 1)
    def _():
        o_ref[...] = x_ref[...] + acc_ref[...]


def _ffn(x, ln, w_gu, w_d, tm):
    n, d = x.shape
    d_ff = w_d.shape[0]
    tf = min(FFN_COL_TILE, d_ff)
    nf = d_ff // tf
    return pl.pallas_call(
        _ffn_kernel,
        out_shape=jax.ShapeDtypeStruct((n, d), F32),
        grid=(n // tm, nf),
        in_specs=[pl.BlockSpec((tm, d), lambda i, f: (i, 0)),
                  pl.BlockSpec((1, d), lambda i, f: (0, 0)),
                  pl.BlockSpec((d, tf), lambda i, f: (0, f)),
                  pl.BlockSpec((d, tf), lambda i, f: (0, f + nf)),
                  pl.BlockSpec((tf, d), lambda i, f: (f, 0))],
        out_specs=pl.BlockSpec((tm, d), lambda i, f: (i, 0)),
        scratch_shapes=[pltpu.VMEM((tm, d), BF16), pltpu.VMEM((tm, d), F32)],
        compiler_params=_params("parallel", "arbitrary"),
    )(x, ln, w_gu, w_gu, w_d)


def _sortable_key(score):
    bits = pltpu.bitcast(score, jnp.int32)
    return jnp.where(bits < 0, bits ^ jnp.int32(0x7FFFFFFF), bits)


def _kth_largest_key(key_ref, k, rows, n_chunks=None):
    width = key_ref.shape[2]

    def count_ge(cand):
        if n_chunks is None:
            ge = jnp.where(key_ref[...] >= cand[None], 1.0, 0.0)
            return jnp.sum(jnp.sum(ge, axis=0), axis=-1, keepdims=True)

        def add_chunk(c, part):
            ge = jnp.where(key_ref[c] >= cand, 1.0, 0.0)
            for w in range(width // LANES):
                part = part + ge[:, w * LANES:(w + 1) * LANES]
            return part

        part = lax.fori_loop(0, n_chunks, add_chunk, jnp.zeros((rows, LANES), F32))
        return jnp.sum(part, axis=-1, keepdims=True)

    zero = jnp.zeros((rows, 1), jnp.int32)
    t0 = jnp.where(count_ge(zero) >= k, zero, jnp.int32(INT32_MIN))

    def body(it, t):
        cand = t | lax.shift_left(jnp.int32(1), jnp.int32(30) - it)
        return jnp.where(count_ge(cand) >= k, cand, t)

    return lax.fori_loop(0, 31, body, t0)


def _split_bf16(x):
    hi = x.astype(BF16)
    return hi, (x - hi.astype(F32)).astype(BF16)


def _index_heads(qi, wq):
    return ([qi[:, h * IDX_DIM:(h + 1) * IDX_DIM].astype(BF16) for h in range(IDX_HEADS)],
            [wq[:, IDX_DIM + h:IDX_DIM + h + 1] for h in range(IDX_HEADS)])


def _index_scores(heads, ki):
    sc = None
    for qh, wh in zip(*heads):
        t = jnp.maximum(lax.dot_general(qh, ki, _NT, preferred_element_type=F32), 0.0) * wh
        sc = t if sc is None else sc + t
    return sc


def _top3_blocks(gate, own, n_blk):
    lane = lax.broadcasted_iota(jnp.int32, gate.shape, 1)
    lane_f = lane.astype(F32)
    gm = jnp.where(lane < own, gate, NEG_INF)
    gm = jnp.where(lane < n_blk, gm, -jnp.inf)
    sel = jnp.zeros(gate.shape, F32)
    for _ in range(MOBA_TOPK):
        mx = jnp.max(gm, axis=-1, keepdims=True)
        first = jnp.min(jnp.where(gm == mx, lane_f, float(LANES)), axis=-1, keepdims=True)
        pick = lane_f == first
        sel = jnp.where(pick, 1.0, sel)
        gm = jnp.where(pick, -jnp.inf, gm)
    return jnp.where(lane < own, sel, jnp.where(lane == own, 1.0, 0.0))


def _top3_blocks_t(gate_t, own, n_blk):
    sub = lax.broadcasted_iota(jnp.int32, gate_t.shape, 0)
    sub_f = sub.astype(F32)
    gm = jnp.where(sub < own, gate_t, NEG_INF)
    gm = jnp.where(sub < n_blk, gm, -jnp.inf)
    sel = jnp.zeros(gate_t.shape, F32)
    for _ in range(MOBA_TOPK):
        mx = jnp.max(gm, axis=0, keepdims=True)
        first = jnp.min(jnp.where(gm == mx, sub_f, float(LANES)), axis=0, keepdims=True)
        pick = sub_f == first
        sel = jnp.where(pick, 1.0, sel)
        gm = jnp.where(pick, -jnp.inf, gm)
    return jnp.where(sub < own, sel, jnp.where(sub == own, 1.0, 0.0))


def _softmax_probs(s, mask, m, l):
    for mk in (mask if isinstance(mask, tuple) else (mask,)):
        s = jnp.where(mk, s, NEG_INF)
    mn = jnp.maximum(m, jnp.max(s, axis=-1, keepdims=True))
    alpha = jnp.exp2(m - mn)
    p = jnp.exp2(s - mn)
    return p, mn, alpha * l + jnp.sum(p, axis=-1, keepdims=True), alpha


def _dsa_select_kernel(qi_ref, wq_ref, kk_ref, mask_ref, key_ref, *, topk):
    i = pl.program_id(1)
    nc, tq, kc = key_ref.shape
    qi = qi_ref[...]
    wq = wq_ref[...]
    qpos = i * tq + lax.broadcasted_iota(jnp.int32, (tq, 1), 0)
    col = lax.broadcasted_iota(jnp.int32, (tq, kc), 1)

    n_live = _div_pow2((i + 1) * tq + kc - 1, kc)
    heads = _index_heads(qi, wq)

    def chunk(c, carry):
        k0 = pl.multiple_of(c * kc, kc)
        ki = kk_ref[pl.ds(k0, kc), :][:, :IDX_DIM].astype(BF16)
        sc = jnp.where(col + k0 <= qpos, _index_scores(heads, ki), NEG_INF)
        key_ref[c] = _sortable_key(sc)
        return carry

    lax.fori_loop(0, n_live, chunk, 0)
    thr = _kth_largest_key(key_ref, topk, tq, n_chunks=n_live)

    def emit(c, carry):
        keep = jnp.where(key_ref[c] >= thr, 1.0, 0.0)
        mask_ref[c] = jnp.where(col + c * kc <= qpos, keep, 0.0).T.astype(BF16)
        return carry

    lax.fori_loop(0, n_live, emit, 0)

    def blank(c, carry):
        mask_ref[c] = jnp.zeros((kc, tq), BF16)
        return carry

    lax.fori_loop(n_live, nc, blank, 0)


def _dsa_select_prompt(qi, kw, b, t):
    tq, kc = ATTN_Q_TILE, DSA_KEY_CHUNK
    nq, nc = t // tq, t // kc
    topk = min(DSA_TOPK_MAX, t // 4)
    assert kc >= topk
    return pl.pallas_call(
        functools.partial(_dsa_select_kernel, topk=topk),
        out_shape=jax.ShapeDtypeStruct((b * nq, nc, kc, tq), BF16),
        grid=(b, nq),
        in_specs=[pl.BlockSpec((tq, IDX_Q), lambda bi, i: (bi * nq + i, 0)),
                  pl.BlockSpec((tq, LANES), lambda bi, i: (bi * nq + i, 0)),
                  pl.BlockSpec((t, LANES), lambda bi, i: (bi, 0))],
        out_specs=pl.BlockSpec((None, nc, kc, tq), lambda bi, i: (bi * nq + i, 0, 0, 0)),
        scratch_shapes=[pltpu.VMEM((nc, tq, kc), jnp.int32)],
        compiler_params=_params("parallel", "arbitrary"),
    )(qi, kw, kw)


def _prompt_attn_kernel(*refs, mode, t, kc):
    if mode == "swa":
        sink_ref, q_ref, k_ref, v_ref, o_ref = refs
    elif mode == "dsa":
        q_ref, k_ref, v_ref, mask_ref, o_ref = refs
    else:
        q_ref, k_ref, v_ref, o_ref, kmean_ref = refs
    i = pl.program_id(1)
    tq = q_ref.shape[0]
    rows = GROUP * tq
    qlane = lax.broadcasted_iota(jnp.int32, (1, rows), 1)
    qpos = i * tq + _mod_pow2(qlane, tq)
    ksub = lax.broadcasted_iota(jnp.int32, (kc, 1), 0)

    if mode == "moba":
        n_blk = t // MOBA_BLOCK
        n_blk8 = -(-n_blk // 8) * 8

        @pl.when(i == 0)
        def _():
            kmean_ref[...] = jnp.zeros_like(kmean_ref)
            for g in range(N_KV_HEADS):
                kg = k_ref[:, g * HEAD_DIM:(g + 1) * HEAD_DIM]
                ksum = jnp.sum(kg.reshape(n_blk, MOBA_BLOCK, HEAD_DIM), axis=1)
                kmean_ref[g, 0:n_blk, :] = ksum * (1.0 / MOBA_BLOCK)

    assert kc % tq == 0
    last = _div_pow2(i * tq, kc)
    first = jnp.maximum(last - 1, 0) if mode == "swa" else 0

    qbs, allows, init = [], [], []
    head = _div_pow2(qlane, tq)
    for g in range(N_KV_HEADS):
        qg = jnp.concatenate(
            [q_ref[:, (g * GROUP + n) * HEAD_DIM:(g * GROUP + n + 1) * HEAD_DIM] for n in range(GROUP)], axis=0)
        qbs.append((qg * (ATTN_SCALE * LOG2E)).astype(BF16))
        if mode == "moba":
            gate_t = lax.dot_general(kmean_ref[g], qg, _NT, precision=lax.Precision.HIGHEST,
                                     preferred_element_type=F32)[:n_blk8]
            allows.append(_top3_blocks_t(gate_t, _div_pow2(qpos, MOBA_BLOCK), n_blk))
        if mode == "swa":
            m0 = jnp.full((1, rows), sink_ref[g * GROUP + GROUP - 1], F32)
            for n in range(GROUP - 1):
                m0 = jnp.where(head == n, sink_ref[g * GROUP + n], m0)
            init += [m0 * LOG2E, jnp.ones((1, rows), F32)]
        else:
            init += [jnp.full((1, rows), NEG_INF, F32), jnp.zeros((1, rows), F32)]
        init.append(jnp.zeros((HEAD_DIM, rows), F32))

    def step(c, carry, diagonal):
        k0 = pl.multiple_of(c * kc, kc)
        kpos = ksub + k0
        if mode == "dsa":
            shared = (jnp.concatenate([mask_ref[c]] * GROUP, axis=1).astype(F32) > 0.5,)
        elif mode == "moba":
            shared = (kpos <= qpos,) if diagonal else ()
        elif diagonal:
            shared = (kpos <= qpos,)
        else:
            shared = (qpos - kpos < WINDOW,)
        out = []
        for g in range(N_KV_HEADS):
            m, l, acc = carry[3 * g:3 * g + 3]
            gsl = slice(g * HEAD_DIM, (g + 1) * HEAD_DIM)
            kb = k_ref[pl.ds(k0, kc), gsl].astype(BF16)
            vb = v_ref[pl.ds(k0, kc), gsl].astype(BF16)
            s = lax.dot_general(kb, qbs[g], _NT, preferred_element_type=F32)
            masks = shared
            if mode == "moba":
                blk_sub = lax.broadcasted_iota(jnp.int32, allows[g].shape, 0)
                picked = jnp.max(jnp.where(blk_sub == c, allows[g], 0.0), axis=0, keepdims=True) > 0.5
                masks = (picked,) + shared
            for mk in masks:
                s = jnp.where(mk, s, NEG_INF)
            mn = jnp.maximum(m, jnp.max(s, axis=0, keepdims=True))
            alpha = jnp.exp2(m - mn)
            p = jnp.exp2(s - mn)
            l = alpha * l + jnp.sum(p, axis=0, keepdims=True)
            pv = lax.dot_general(vb, p.astype(BF16), _TN, preferred_element_type=F32)
            out += [mn, l, alpha * acc + pv]
        return tuple(out)

    carry = lax.fori_loop(first, last, functools.partial(step, diagonal=False), tuple(init))
    carry = step(last, carry, True)
    for g in range(N_KV_HEADS):
        o_t = carry[3 * g + 2] / carry[3 * g + 1]
        for n in range(GROUP):
            hsl = slice((g * GROUP + n) * HEAD_DIM, (g * GROUP + n + 1) * HEAD_DIM)
            o_ref[:, hsl] = o_t[:, n * tq:(n + 1) * tq].T


def _prompt_attn(qkv, b, t, mode, mask=None, sinks=None):
    tq = ATTN_Q_TILE
    nq = t // tq
    kc = {"dsa": DSA_KEY_CHUNK, "moba": MOBA_BLOCK, "swa": WINDOW}[mode]
    in_specs = [pl.BlockSpec((tq, Q_DIM), lambda bi, i: (bi * nq + i, 0)),
                pl.BlockSpec((t, KV_DIM), lambda bi, i: (bi, Q_DIM // KV_DIM)),
                pl.BlockSpec((t, KV_DIM), lambda bi, i: (bi, Q_DIM // KV_DIM + 1))]
    args = [qkv, qkv, qkv]
    scratch = []
    if mode == "dsa":
        nc = t // kc
        in_specs.append(pl.BlockSpec((None, nc, kc, tq), lambda bi, i: (bi * nq + i, 0, 0, 0)))
        args.append(mask)
    elif mode == "moba":
        assert t % MOBA_BLOCK == 0 and t // MOBA_BLOCK <= LANES
        scratch.append(pltpu.VMEM((N_KV_HEADS, LANES, HEAD_DIM), F32))
    else:
        assert tq == WINDOW
        in_specs.insert(0, pl.BlockSpec(memory_space=pltpu.SMEM))
        args.insert(0, sinks)
    return pl.pallas_call(
        functools.partial(_prompt_attn_kernel, mode=mode, t=t, kc=kc),
        out_shape=jax.ShapeDtypeStruct((b * t, Q_DIM), F32),
        grid=(b, nq),
        in_specs=in_specs,
        out_specs=pl.BlockSpec((tq, Q_DIM), lambda bi, i: (bi * nq + i, 0)),
        scratch_shapes=scratch,
        compiler_params=_params("parallel", "arbitrary"),
    )(*args)


def _page_specs(shape_tail, n_pages, pages_per_step):
    def spec(j):
        def index(bi, c, pt):
            page = jnp.minimum(c * pages_per_step + j, n_pages - 1)
            return (pt[bi * n_pages + page],) + (0,) * len(shape_tail)
        return pl.BlockSpec((None,) + shape_tail, index)
    return [spec(j) for j in range(pages_per_step)]


def _dsa_sample_select_kernel(pt_ref, qi_ref, w_ref, *refs, pages, topk, s_len):
    ik_refs = refs[:pages]
    knew_ref, mask_ref, key_ref = refs[pages:]
    c = pl.program_id(1)
    last = pl.num_programs(1) - 1
    nsteps, srows, width = key_ref.shape
    rows = mask_ref.shape[1]
    qi = qi_ref[...]
    w = w_ref[...]

    q_hi, q_lo = _split_bf16(qi)

    def scores(ki_t):
        k_hi, k_lo = _split_bf16(ki_t)
        d = (jnp.dot(q_hi, k_hi, preferred_element_type=F32) + jnp.dot(q_hi, k_lo, preferred_element_type=F32)
             + jnp.dot(q_lo, k_hi, preferred_element_type=F32))
        r = jnp.maximum(d, 0.0) * w
        srow = lax.broadcasted_iota(jnp.int32, (srows, ki_t.shape[1]), 0)
        sc = jnp.zeros((srows, ki_t.shape[1]), F32)
        for s in range(s_len):
            tot = jnp.sum(r[s * IDX_HEADS:(s + 1) * IDX_HEADS], axis=0, keepdims=True)
            sc = jnp.where(srow == s, tot, sc)
        return sc

    srow_w = lax.broadcasted_iota(jnp.int32, (srows, width), 0)
    new_ok = (lax.broadcasted_iota(jnp.int32, (srows, width), 1) <= srow_w) & (srow_w < s_len)

    @pl.when(c < last)
    def _():
        key_ref[c] = _sortable_key(scores(jnp.concatenate([r[...] for r in ik_refs], axis=1)))

    @pl.when(c == last)
    def _():
        sc = jnp.concatenate([scores(knew_ref[...]), jnp.zeros((srows, width - PAGE_SIZE), F32)], axis=1)
        key_ref[c] = _sortable_key(jnp.where(new_ok, sc, NEG_INF))
        thr = _kth_largest_key(key_ref, topk, srows)
        tok = _mod_pow2(lax.broadcasted_iota(jnp.int32, (rows, width), 0), s_len)
        for cc in range(nsteps):
            keep = key_ref[cc] >= thr
            if cc == nsteps - 1:
                keep = keep & new_ok
            keep = jnp.where(keep, 1.0, 0.0)
            out = jnp.zeros((rows, width), F32)
            for s in range(s_len):
                out = jnp.where(tok == s, keep[s:s + 1, :], out)
            mask_ref[0, :, cc * width:(cc + 1) * width] = out.astype(BF16)


def _dsa_select_sample(pt_flat, qi_s, w_s, cache_ik_t, ki_new_t, n_pages, s_len):
    db = qi_s.shape[0]
    rows = GROUP * s_len
    srows = 8
    pages = min(PAGES_PER_STEP, n_pages)
    assert n_pages % pages == 0 and s_len <= srows and pages > 1
    nsteps = n_pages // pages + 1
    width = pages * PAGE_SIZE
    lp = nsteps * width
    bsel = lambda bi, c, pt: (bi, 0, 0)
    grid_spec = pltpu.PrefetchScalarGridSpec(
        num_scalar_prefetch=1,
        grid=(db, nsteps),
        in_specs=[pl.BlockSpec((None, s_len * IDX_HEADS, IDX_DIM), bsel),
                  pl.BlockSpec((None, s_len * IDX_HEADS, 1), bsel)]
        + _page_specs((IDX_DIM, PAGE_SIZE), n_pages, pages)
        + [pl.BlockSpec((None, IDX_DIM, PAGE_SIZE), bsel)],
        out_specs=pl.BlockSpec((None, 1, rows, lp), lambda bi, c, pt: (bi, 0, 0, 0)),
        scratch_shapes=[pltpu.VMEM((nsteps, srows, width), jnp.int32)],
    )
    topk = min(DSA_TOPK_MAX, (n_pages * PAGE_SIZE + s_len) // 4)
    return pl.pallas_call(
        functools.partial(_dsa_sample_select_kernel, pages=pages, topk=topk, s_len=s_len),
        out_shape=jax.ShapeDtypeStruct((db, 1, rows, lp), BF16),
        grid_spec=grid_spec,
        compiler_params=_params("parallel", "arbitrary"),
    )(pt_flat, qi_s, w_s, *([cache_ik_t] * pages), ki_new_t)


def _moba_block_mean_kernel(pt_ref, *refs, pages):
    k_refs = refs[:pages]
    o_ref = refs[pages]
    per_blk = MOBA_BLOCK // PAGE_SIZE
    for j in range(pages // per_blk):
        tot = None
        for u in range(per_blk):
            page = k_refs[j * per_blk + u][...]
            part = jnp.sum(page.reshape(page.shape[0] // 8, 8, HEAD_DIM), axis=0)
            tot = part if tot is None else tot + part
        o_ref[j * 8:(j + 1) * 8, :] = tot * (1.0 / MOBA_BLOCK)


def _moba_block_means(pt_flat, cache_k, db, n_pages):
    pages = MOBA_SUM_PAGES
    assert n_pages % pages == 0 and 2 * N_KV_HEADS == 8
    per_step = pages * PAGE_SIZE // MOBA_BLOCK
    grid_spec = pltpu.PrefetchScalarGridSpec(
        num_scalar_prefetch=1,
        grid=(db, n_pages // pages),
        in_specs=_page_specs((PAGE_SIZE * N_KV_HEADS, HEAD_DIM), n_pages, pages),
        out_specs=pl.BlockSpec((None, per_step * 8, HEAD_DIM), lambda bi, c, pt: (bi, c, 0)),
    )
    return pl.pallas_call(
        functools.partial(_moba_block_mean_kernel, pages=pages),
        out_shape=jax.ShapeDtypeStruct((db, n_pages * PAGE_SIZE // MOBA_BLOCK * 8, HEAD_DIM), F32),
        grid_spec=grid_spec,
        compiler_params=_params("parallel", "arbitrary"),
    )(pt_flat, *([cache_k] * pages))


def _moba_sample_gate_kernel(q_ref, kmean_ref, allow_ref, *, n_past_blk, n_blk, s_len, past_len):
    rows = q_ref.shape[1]
    tok = _mod_pow2(lax.broadcasted_iota(jnp.int32, (rows, 1), 0), s_len)
    own = _div_pow2(past_len + tok, MOBA_BLOCK)
    for g in range(N_KV_HEADS):
        km = (kmean_ref[pl.ds(g, n_past_blk, stride=8), :]
              + kmean_ref[pl.ds(g + N_KV_HEADS, n_past_blk, stride=8), :])
        km = jnp.concatenate([km, jnp.zeros((LANES - n_past_blk, HEAD_DIM), F32)], axis=0)
        gate = lax.dot_general(q_ref[g], km, _NT, precision=lax.Precision.HIGHEST, preferred_element_type=F32)
        allow_ref[g] = _top3_blocks(gate, own, n_blk)


def _moba_sample_gate(qs, kmean, n_pages, s_len):
    db, _, rows, _ = qs.shape
    past_len = n_pages * PAGE_SIZE
    n_past_blk = kmean.shape[1] // 8
    n_blk = -(-(past_len + s_len) // MOBA_BLOCK)
    assert past_len % MOBA_BLOCK == 0 and s_len <= MOBA_BLOCK and n_blk <= LANES
    return pl.pallas_call(
        functools.partial(_moba_sample_gate_kernel, n_past_blk=n_past_blk, n_blk=n_blk, s_len=s_len,
                          past_len=past_len),
        out_shape=jax.ShapeDtypeStruct((db, N_KV_HEADS, rows, LANES), F32),
        grid=(db,),
        in_specs=[pl.BlockSpec((None, N_KV_HEADS, rows, HEAD_DIM), lambda bi: (bi, 0, 0, 0)),
                  pl.BlockSpec((None, n_past_blk * 8, HEAD_DIM), lambda bi: (bi, 0, 0))],
        out_specs=pl.BlockSpec((None, N_KV_HEADS, rows, LANES), lambda bi: (bi, 0, 0, 0)),
        compiler_params=_params("parallel"),
    )(qs, kmean)


def _paged_attn_kernel(pt_ref, *refs, pages, mode, use_sink, s_len, past_len):
    if use_sink:
        sink_ref, refs = refs[0], refs[1:]
    q_ref = refs[0]
    k_refs = refs[1:1 + pages]
    v_refs = refs[1 + pages:1 + 2 * pages]
    knew_ref, vnew_ref, sel_ref, o_ref, m_ref, l_ref, acc_ref = refs[1 + 2 * pages:]
    c = pl.program_id(1)
    last = pl.num_programs(1) - 1
    rows = q_ref.shape[1]
    all_rows = N_KV_HEADS * rows
    row = lax.broadcasted_iota(jnp.int32, (all_rows, 1), 0)
    tok = _mod_pow2(row, s_len)

    @pl.when(c == 0)
    def _():
        if use_sink:
            head = _div_pow2(row, s_len)
            m0 = jnp.full((all_rows, 1), sink_ref[N_HEADS - 1], F32)
            for h in range(N_HEADS - 1):
                m0 = jnp.where(head == h, sink_ref[h], m0)
            m_ref[...] = m0 * LOG2E
            l_ref[...] = jnp.ones((all_rows, 1), F32)
        else:
            m_ref[...] = jnp.full((all_rows, 1), NEG_INF, F32)
            l_ref[...] = jnp.zeros((all_rows, 1), F32)
        acc_ref[...] = jnp.zeros((all_rows, HEAD_DIM), F32)

    def block_allowed(blk):
        lane = lax.broadcasted_iota(jnp.int32, (all_rows, LANES), 1)
        allow = sel_ref[...].reshape(all_rows, LANES)
        return jnp.max(jnp.where(lane == blk, allow, 0.0), axis=-1, keepdims=True) > 0.5

    def head_rows(page_refs, g):
        return jnp.concatenate([r[pl.ds(g, PAGE_SIZE, stride=N_KV_HEADS), :] for r in page_refs],
                               axis=0).astype(BF16)

    def update(k_pages, v_pages, mask):
        qb = (q_ref[...].reshape(all_rows, HEAD_DIM) * (ATTN_SCALE * LOG2E)).astype(BF16)
        s = jnp.concatenate(
            [lax.dot_general(qb[g * rows:(g + 1) * rows], head_rows(k_pages, g), _NT, preferred_element_type=F32)
             for g in range(N_KV_HEADS)], axis=0)
        p, m, l, alpha = _softmax_probs(s, mask, m_ref[...], l_ref[...])
        pb = p.astype(BF16)
        pv = jnp.concatenate(
            [jnp.dot(pb[g * rows:(g + 1) * rows], head_rows(v_pages, g), preferred_element_type=F32)
             for g in range(N_KV_HEADS)], axis=0)
        m_ref[...] = m
        l_ref[...] = l
        acc_ref[...] = alpha * acc_ref[...] + pv

    @pl.when(c < last)
    def _():
        if mode == "mask":
            mask = jnp.concatenate([sel_ref[0]] * N_KV_HEADS, axis=0).astype(F32) > 0.5
        else:
            mask = jnp.concatenate(
                [jnp.broadcast_to(block_allowed(_div_pow2(c * pages + j, MOBA_BLOCK // PAGE_SIZE)),
                                  (all_rows, PAGE_SIZE)) for j in range(pages)], axis=1)
        update(k_refs, v_refs, mask)

    @pl.when(c == last)
    def _():
        if mode == "mask":
            mask = jnp.concatenate([sel_ref[0][:, :PAGE_SIZE]] * N_KV_HEADS, axis=0).astype(F32) > 0.5
        else:
            col = lax.broadcasted_iota(jnp.int32, (all_rows, PAGE_SIZE), 1)
            mask = (block_allowed(past_len // MOBA_BLOCK), col <= tok)
        update([knew_ref], [vnew_ref], mask)
        o_ref[...] = (acc_ref[...] / l_ref[...]).reshape(N_KV_HEADS, rows, HEAD_DIM)


def _paged_attn(pt_flat, qs, cache_k, cache_v, k_new, v_new, sel, n_pages, s_len, mode, sinks=None):
    db, _, rows, _ = qs.shape
    pages = min(PAGES_PER_STEP, n_pages)
    assert n_pages % pages == 0
    nsteps = n_pages // pages + 1
    width = pages * PAGE_SIZE
    bsel3 = lambda bi, c, pt: (bi, 0, 0)
    bsel4 = lambda bi, c, pt: (bi, 0, 0, 0)
    if mode == "mask":
        sel_spec = pl.BlockSpec((None, 1, rows, width), lambda bi, c, pt: (bi, 0, 0, c))
    else:
        assert (n_pages * PAGE_SIZE) % MOBA_BLOCK == 0 and s_len <= MOBA_BLOCK
        sel_spec = pl.BlockSpec((None, N_KV_HEADS, rows, LANES), bsel4)
    page_shape = (PAGE_SIZE * N_KV_HEADS, HEAD_DIM)
    in_specs = ([pl.BlockSpec((None, N_KV_HEADS, rows, HEAD_DIM), bsel4)]
                + _page_specs(page_shape, n_pages, pages) + _page_specs(page_shape, n_pages, pages)
                + [pl.BlockSpec((None,) + page_shape, bsel3)] * 2 + [sel_spec])
    args = [qs] + [cache_k] * pages + [cache_v] * pages + [k_new, v_new, sel]
    if sinks is not None:
        in_specs.insert(0, pl.BlockSpec(memory_space=pltpu.SMEM))
        args.insert(0, sinks)
    grid_spec = pltpu.PrefetchScalarGridSpec(
        num_scalar_prefetch=1,
        grid=(db, nsteps),
        in_specs=in_specs,
        out_specs=pl.BlockSpec((None, N_KV_HEADS, rows, HEAD_DIM), bsel4),
        scratch_shapes=[pltpu.VMEM((N_KV_HEADS * rows, 1), F32), pltpu.VMEM((N_KV_HEADS * rows, 1), F32),
                        pltpu.VMEM((N_KV_HEADS * rows, HEAD_DIM), F32)],
    )
    return pl.pallas_call(
        functools.partial(_paged_attn_kernel, pages=pages, mode=mode, use_sink=sinks is not None,
                          s_len=s_len, past_len=n_pages * PAGE_SIZE),
        out_shape=jax.ShapeDtypeStruct((db, N_KV_HEADS, rows, HEAD_DIM), F32),
        grid_spec=grid_spec,
        compiler_params=_params("parallel", "arbitrary"),
    )(pt_flat, *args)


def _pad_rows(a, rows):
    return jnp.pad(a, ((0, 0), (0, rows - a.shape[1]), (0, 0)))


def _as_pages(cache):
    assert cache.shape[1:] == (PAGE_SIZE, N_KV_HEADS, HEAD_DIM), cache.shape
    return cache.reshape(cache.shape[0], PAGE_SIZE * N_KV_HEADS, HEAD_DIM)


def kernel(x_prompt, x_sample, cache_k_0, cache_v_0, cache_ik_0, cache_k_1, cache_v_1, state_wk_2, state_wv_2,
           cache_k_3, cache_v_3, cache_ik_3, page_table, ln_attn, w_qkv, q_norm, k_norm, w_o, w_idx, idx_ln_g,
           idx_ln_b, sinks, ln_ffn, w_gate_up, w_down):
    b, t, d = x_prompt.shape
    db, s_len, _ = x_sample.shape
    n_pages = page_table.shape[1]
    past_len = n_pages * PAGE_SIZE
    depth = w_qkv.shape[0]
    n_p = b * t
    n = n_p + db * s_len
    tm = _row_tile(n)
    rows = GROUP * s_len
    assert t % DSA_KEY_CHUNK == 0 and t % ATTN_Q_TILE == 0 and rows % 8 == 0

    caches = {0: (cache_k_0, cache_v_0, cache_ik_0), 1: (cache_k_1, cache_v_1), 2: (state_wk_2, state_wv_2),
              3: (cache_k_3, cache_v_3, cache_ik_3)}
    pt_flat = page_table.reshape(-1).astype(jnp.int32)
    win_pt = jnp.arange(db, dtype=jnp.int32)

    pos = jnp.concatenate([jnp.tile(jnp.arange(t, dtype=jnp.int32), b),
                           jnp.tile(past_len + jnp.arange(s_len, dtype=jnp.int32), db)])
    tabs_main = _rope_tables(pos, ROPE_DIM // 2, HEAD_DIM)
    tabs_iq = _rope_tables(pos, IDX_ROPE_DIM // 2, IDX_DIM)
    tabs_ik = _rope_tables(pos, IDX_ROPE_DIM // 2, LANES)

    tok = (jnp.arange(rows, dtype=jnp.int32) % s_len)[:, None]
    col = jnp.arange(PAGE_SIZE, dtype=jnp.int32)[None, :]
    swa_mask = jnp.concatenate([col > tok + (WINDOW - PAGE_SIZE), col <= tok], axis=1)
    swa_mask = jnp.broadcast_to(swa_mask.astype(BF16)[None, None], (db, 1, rows, 2 * PAGE_SIZE))

    x = jnp.concatenate([x_prompt.reshape(n_p, d), x_sample.reshape(db * s_len, d)], axis=0)
    outs = {}
    for i in range(depth):
        kind, j = i % 3, i // 3
        gain = jnp.concatenate([jnp.tile(q_norm[i], N_HEADS), jnp.tile(k_norm[i], N_KV_HEADS),
                                jnp.ones((KV_DIM,), F32)])[None, :]
        ln_a = ln_attn[i][None, :]
        qkv = _qkv_proj(x, ln_a, w_qkv[i].astype(BF16), gain, tabs_main, tm)
        k_all = qkv[:, Q_DIM:Q_DIM + KV_DIM]
        v_all = qkv[:, Q_DIM + KV_DIM:]
        k_p = k_all[:n_p].reshape(b, t, N_KV_HEADS, HEAD_DIM)
        v_p = v_all[:n_p].reshape(b, t, N_KV_HEADS, HEAD_DIM)
        k_s = k_all[n_p:].reshape(db, s_len, N_KV_HEADS, HEAD_DIM)
        v_s = v_all[n_p:].reshape(db, s_len, N_KV_HEADS, HEAD_DIM)
        k_new = _pad_rows(k_all[n_p:].reshape(db, s_len * N_KV_HEADS, HEAD_DIM), PAGE_SIZE * N_KV_HEADS)
        v_new = _pad_rows(v_all[n_p:].reshape(db, s_len * N_KV_HEADS, HEAD_DIM), PAGE_SIZE * N_KV_HEADS)
        qs = qkv[n_p:, :Q_DIM].reshape(db, s_len, N_KV_HEADS, GROUP, HEAD_DIM)
        qs = qs.transpose(0, 2, 3, 1, 4).reshape(db, N_KV_HEADS, rows, HEAD_DIM)

        if kind == 0:
            ck, cv, cik = caches[i]
            w_i = jnp.pad(w_idx[j], ((0, 0), (0, IDX_PROJ_PAD - IDX_PROJ))).astype(BF16)
            lg = jnp.pad(idx_ln_g[j], (0, LANES - IDX_DIM))[None, :]
            lb = jnp.pad(idx_ln_b[j], (0, LANES - IDX_DIM))[None, :]
            qi, kw = _idx_proj(x, ln_a, w_i, lg, lb, tabs_iq, tabs_ik, tm)
            mask_p = _dsa_select_prompt(qi, kw, b, t)
            o_p = _prompt_attn(qkv, b, t, "dsa", mask=mask_p)
            qi_s = qi[n_p:].reshape(db, s_len * IDX_HEADS, IDX_DIM)
            w_s = kw[n_p:, IDX_DIM:IDX_DIM + IDX_HEADS].reshape(db, s_len * IDX_HEADS, 1)
            ki_new_t = _pad_rows(kw[n_p:, :IDX_DIM].reshape(db, s_len, IDX_DIM), PAGE_SIZE).transpose(0, 2, 1)
            mask_s = _dsa_select_sample(pt_flat, qi_s, w_s, cik.transpose(0, 2, 1), ki_new_t, n_pages, s_len)
            o_s = _paged_attn(pt_flat, qs, _as_pages(ck), _as_pages(cv), k_new, v_new, mask_s, n_pages, s_len,
                              "mask")
            ik_p = kw[:n_p, :IDX_DIM].reshape(b, t, IDX_DIM)
            ik_s = kw[n_p:, :IDX_DIM].reshape(db, s_len, IDX_DIM)
            outs[i] = (k_p, v_p, ik_p, k_s, v_s, ik_s)
        elif kind == 1:
            ck, cv = caches[i]
            ck2 = _as_pages(ck)
            cv2 = _as_pages(cv)
            o_p = _prompt_attn(qkv, b, t, "moba")
            kmean = _moba_block_means(pt_flat, ck2, db, n_pages)
            allow = _moba_sample_gate(qs, kmean, n_pages, s_len)
            o_s = _paged_attn(pt_flat, qs, ck2, cv2, k_new, v_new, allow, n_pages, s_len, "block")
            outs[i] = (k_p, v_p, k_s, v_s)
        else:
            bk, bv = caches[i]
            o_p = _prompt_attn(qkv, b, t, "swa", sinks=sinks[j])
            o_s = _paged_attn(win_pt, qs, _as_pages(bk), _as_pages(bv), k_new, v_new, swa_mask, 1, s_len, "mask",
                              sinks=sinks[j])
            outs[i] = (k_p[:, -WINDOW:], v_p[:, -WINDOW:],
                       jnp.concatenate([bk, k_s], axis=1)[:, -WINDOW:],
                       jnp.concatenate([bv, v_s], axis=1)[:, -WINDOW:])

        o_s = o_s.reshape(db, N_KV_HEADS, GROUP, s_len, HEAD_DIM).transpose(0, 3, 1, 2, 4).reshape(db * s_len, Q_DIM)
        o = jnp.concatenate([o_p, o_s], axis=0)
        x = _out_proj(o, w_o[i].astype(BF16), x, tm)
        x = _ffn(x, ln_ffn[i][None, :], w_gate_up[i].astype(BF16), w_down[i].astype(BF16), tm)

    y_p = x[:n_p].reshape(b, t, d)
    y_s = x[n_p:].reshape(db, s_len, d)
    return (y_p, y_s) + outs[0] + outs[1] + outs[2] + outs[3]
```
